```python
import math
import jax, jax.numpy as jnp
from jax import lax
import numpy as np

D_MODEL = 2048
BATCH = 2
SEQ = 16384
DEPTH = 1

SB_HEADS = 8
SB_HEAD_DIM = 128
SB_WIDTH = SB_HEADS * SB_HEAD_DIM
Q_BLOCK = 128
LRU_WIDTH = 1024
LRU_BLOCKS = 8
LRU_BLOCK_DIM = LRU_WIDTH // LRU_BLOCKS
LRU_C = 8.0
CONV_W = 4
D_FF = 4 * D_MODEL
NORM_EPS = 1e-6
IN_SPLITS = [SB_WIDTH, SB_WIDTH, SB_WIDTH, LRU_WIDTH, LRU_WIDTH, D_MODEL, D_MODEL]
D_IN = sum(IN_SPLITS)
N_MOD = 6

kernel_name = "hybrid_stickbreak_rglru_block"


def rms_norm(x, g):
    xf = x.astype(jnp.float32)
    y = xf * lax.rsqrt(jnp.mean(xf * xf, axis=-1, keepdims=True) + NORM_EPS)
    return (y * g.astype(jnp.float32)).astype(x.dtype)


def split_cols(y, sizes):
    idx = np.cumsum(sizes)[:-1].tolist()
    return jnp.split(y, idx, axis=-1)


def stick_breaking_attention(q, k, v):
    B, H, S, Dh = q.shape
    nb = S // Q_BLOCK
    scale = 1.0 / math.sqrt(Dh)
    qb = q.reshape(B, H, nb, Q_BLOCK, Dh).transpose(2, 0, 1, 3, 4)
    key_pos = jnp.arange(S)

    def one_block(args):
        qi, i = args
        z = jnp.einsum('bhqd,bhkd->bhqk', qi, k).astype(jnp.float32) * scale
        q_pos = i * Q_BLOCK + jnp.arange(Q_BLOCK)
        mask = key_pos[None, :] < q_pos[:, None]
        log_beta = jax.nn.log_sigmoid(z)
        log_one_minus = jnp.where(mask, jax.nn.log_sigmoid(-z), 0.0)
        tail = lax.cumsum(log_one_minus, axis=3, reverse=True) - log_one_minus
        w = jnp.where(mask, jnp.exp(log_beta + tail), 0.0)
        return jnp.einsum('bhqk,bhkd->bhqd', w.astype(v.dtype), v)

    out = lax.map(one_block, (qb, jnp.arange(nb)))
    return out.transpose(1, 2, 0, 3, 4).reshape(B, H, S, Dh)


def causal_depthwise_conv(u, w, b):
    S = u.shape[1]
    up = jnp.pad(u, ((0, 0), (CONV_W - 1, 0), (0, 0)))
    out = b
    for tap in range(CONV_W):
        out = out + up[:, tap:tap + S, :] * w[tap]
    return out


def rg_lru(u, w_rg, b_rg, w_ig, b_ig, lam):
    B, S, W = u.shape
    ub = u.reshape(B, S, LRU_BLOCKS, LRU_BLOCK_DIM)
    r = jax.nn.sigmoid((jnp.einsum('bsnd,nde->bsne', ub, w_rg).reshape(B, S, W) + b_rg).astype(jnp.float32))
    i = jax.nn.sigmoid((jnp.einsum('bsnd,nde->bsne', ub, w_ig).reshape(B, S, W) + b_ig).astype(jnp.float32))
    log_a = LRU_C * r * jax.nn.log_sigmoid(lam.astype(jnp.float32))
    a = jnp.exp(log_a)
    mult = jnp.sqrt(-jnp.expm1(2.0 * log_a))
    bterm = mult * (i * u.astype(jnp.float32))

    def combine(e1, e2):
        a1, b1 = e1
        a2, b2 = e2
        return a1 * a2, a2 * b1 + b2

    _, h = lax.associative_scan(combine, (a, bterm), axis=1)
    return h.astype(u.dtype)


def setup_inputs(seed: int = 0) -> dict:
    key = jax.random.key(seed)
    ks = jax.random.split(key, 24)
    f32 = jnp.float32
    nrm = lambda k, shape, fan: jax.random.normal(k, shape, f32) * (fan ** -0.5)
    u = jax.random.uniform(ks[14], (DEPTH, LRU_WIDTH), f32, 0.9, 0.999)
    return {
        "x": jax.random.normal(ks[0], (BATCH, SEQ, D_MODEL), f32),
        "c": jax.random.normal(ks[1], (BATCH, D_MODEL), f32),
        "w_ada": nrm(ks[2], (DEPTH, D_MODEL, N_MOD * D_MODEL), D_MODEL) * 0.5,
        "b_ada": 0.02 * jax.random.normal(ks[3], (DEPTH, N_MOD * D_MODEL), f32),
        "norm1_g": 1.0 + 0.02 * jax.random.normal(ks[4], (DEPTH, D_MODEL), f32),
        "w_in": nrm(ks[5], (DEPTH, D_MODEL, D_IN), D_MODEL),
        "b_in": 0.02 * jax.random.normal(ks[6], (DEPTH, D_IN), f32),
        "q_norm_g": 1.0 + 0.02 * jax.random.normal(ks[7], (DEPTH, SB_HEAD_DIM), f32),
        "k_norm_g": 1.0 + 0.02 * jax.random.normal(ks[8], (DEPTH, SB_HEAD_DIM), f32),
        "conv_w": nrm(ks[9], (DEPTH, CONV_W, LRU_WIDTH), CONV_W),
        "conv_b": 0.02 * jax.random.normal(ks[10], (DEPTH, LRU_WIDTH), f32),
        "w_rg": nrm(ks[11], (DEPTH, LRU_BLOCKS, LRU_BLOCK_DIM, LRU_BLOCK_DIM), LRU_BLOCK_DIM),
        "b_rg": 0.02 * jax.random.normal(ks[12], (DEPTH, LRU_WIDTH), f32),
        "w_ig": nrm(ks[13], (DEPTH, LRU_BLOCKS, LRU_BLOCK_DIM, LRU_BLOCK_DIM), LRU_BLOCK_DIM),
        "b_ig": 0.02 * jax.random.normal(ks[15], (DEPTH, LRU_WIDTH), f32),
        "lru_lambda": jnp.log(u) - jnp.log1p(-u),
        "w_branch_a": nrm(ks[16], (DEPTH, SB_WIDTH, D_MODEL), SB_WIDTH),
        "w_branch_b": nrm(ks[17], (DEPTH, LRU_WIDTH, D_MODEL), LRU_WIDTH),
        "w_out": nrm(ks[18], (DEPTH, D_MODEL, D_MODEL), D_MODEL),
        "norm2_g": 1.0 + 0.02 * jax.random.normal(ks[19], (DEPTH, D_MODEL), f32),
        "w_mlp_in": nrm(ks[20], (DEPTH, D_MODEL, D_FF), D_MODEL),
        "w_mlp_out": nrm(ks[21], (DEPTH, D_FF, D_MODEL), D_FF),
    }


def reference(x, c, w_ada, b_ada, norm1_g, w_in, b_in, q_norm_g, k_norm_g, conv_w, conv_b,
              w_rg, b_rg, w_ig, b_ig, lru_lambda, w_branch_a, w_branch_b, w_out, norm2_g,
              w_mlp_in, w_mlp_out):
    B, S, D = x.shape
    c_act = jax.nn.silu(c)
    for l in range(DEPTH):
        mod = c_act @ w_ada[l] + b_ada[l]
        shift1, scale1, gate1, shift2, scale2, gate2 = [m[:, None, :] for m in jnp.split(mod, N_MOD, axis=-1)]

        h = rms_norm(x, norm1_g[l]) * (1.0 + scale1) + shift1
        proj = h @ w_in[l] + b_in[l]
        q, k, v, lru_x, lru_gate, gate_a, gate_b = split_cols(proj, IN_SPLITS)

        to_heads = lambda t: t.reshape(B, S, SB_HEADS, SB_HEAD_DIM).transpose(0, 2, 1, 3)
        qh = rms_norm(to_heads(q), q_norm_g[l])
        kh = rms_norm(to_heads(k), k_norm_g[l])
        vh = to_heads(v)
        att = stick_breaking_attention(qh, kh, vh)
        att = att.transpose(0, 2, 1, 3).reshape(B, S, SB_WIDTH)
        y_a = att @ w_branch_a[l]

        u = causal_depthwise_conv(lru_x, conv_w[l], conv_b[l])
        rec = rg_lru(u, w_rg[l], b_rg[l], w_ig[l], b_ig[l], lru_lambda[l])
        y_b = (rec * jax.nn.gelu(lru_gate)) @ w_branch_b[l]

        merged = jax.nn.sigmoid(gate_a) * y_a + jax.nn.sigmoid(gate_b) * y_b
        x = x + gate1 * (merged @ w_out[l])

        h2 = rms_norm(x, norm2_g[l]) * (1.0 + scale2) + shift2
        ff = jnp.square(jax.nn.relu(h2 @ w_mlp_in[l])) @ w_mlp_out[l]
        x = x + gate2 * ff
    return x
```

```python
import functools
import math

import jax
import jax.numpy as jnp
from jax import lax
from jax.experimental import pallas as pl
from jax.experimental.pallas import tpu as pltpu

F32 = jnp.float32
BF16 = jnp.bfloat16

NORM_EPS = 1e-6
LRU_C = 8.0
HEAD_DIM = 128
N_MOD = 6
SUBLANES = 8
F32_EXP_UNDERFLOW = 104.0
VMEM_LIMIT = 56 * 1024 * 1024


def _params(sem, vmem=VMEM_LIMIT):
    return pltpu.CompilerParams(dimension_semantics=sem, vmem_limit_bytes=vmem)


def _dot(a, b):
    return jnp.dot(a, b, preferred_element_type=F32)


def _ada_kernel(c_ref, w_ref, b_ref, o_ref):
    ca = jax.nn.silu(c_ref[...]).astype(BF16)
    o_ref[...] = _dot(ca, w_ref[...].astype(BF16)) + b_ref[...]


def _ada(c_pad, w_ada, b_ada, tn=1024):
    m, d = c_pad.shape
    n = w_ada.shape[1]
    return pl.pallas_call(
        _ada_kernel,
        grid=(n // tn,),
        in_specs=[pl.BlockSpec((m, d), lambda j: (0, 0)),
                  pl.BlockSpec((d, tn), lambda j: (0, j)),
                  pl.BlockSpec((1, tn), lambda j: (0, j))],
        out_specs=pl.BlockSpec((m, tn), lambda j: (0, j)),
        out_shape=jax.ShapeDtypeStruct((m, n), F32),
        compiler_params=_params(("arbitrary",)),
        name="ada",
    )(c_pad, w_ada, b_ada)


def _inproj_kernel(x_ref, mod_ref, g1_ref, w_ref, b_ref, qg_ref, kg_ref, obf_ref, of_ref, h_scr, *, n_heads):
    j = pl.program_id(1)

    @pl.when(j == 0)
    def _():
        x = x_ref[...]
        ms = jnp.mean(x * x, axis=-1, keepdims=True)
        y = x * lax.rsqrt(ms + NORM_EPS) * g1_ref[...]
        h_scr[...] = (y * (1.0 + mod_ref[1:2, :]) + mod_ref[0:1, :]).astype(BF16)

    acc = _dot(h_scr[...], w_ref[...]) + b_ref[...]

    @pl.when(j < 2)
    def _():
        g = jnp.where(j == 0, qg_ref[...], kg_ref[...])
        for hh in range(n_heads):
            sl = slice(hh * HEAD_DIM, (hh + 1) * HEAD_DIM)
            a = acc[:, sl]
            ms = jnp.mean(a * a, axis=-1, keepdims=True)
            obf_ref[:, sl] = (a * lax.rsqrt(ms + NORM_EPS) * g).astype(BF16)

    @pl.when(j == 2)
    def _():
        obf_ref[...] = acc.astype(BF16)

    @pl.when(j >= 3)
    def _():
        of_ref[...] = acc


def _inproj(x2, mod, g1, w_in, b_in, qg, kg, seq, tm=1024, tn=1024):
    t, d = x2.shape
    n = w_in.shape[1]
    nj = n // tn
    tiles_per_batch = seq // tm
    kern = functools.partial(_inproj_kernel, n_heads=tn // HEAD_DIM)
    return pl.pallas_call(
        kern,
        grid=(t // tm, nj),
        in_specs=[pl.BlockSpec((tm, d), lambda i, j: (i, 0)),
                  pl.BlockSpec((None, N_MOD, d), lambda i, j: (i // tiles_per_batch, 0, 0)),
                  pl.BlockSpec((1, d), lambda i, j: (0, 0)),
                  pl.BlockSpec((d, tn), lambda i, j: (0, j)),
                  pl.BlockSpec((1, tn), lambda i, j: (0, j)),
                  pl.BlockSpec((1, HEAD_DIM), lambda i, j: (0, 0)),
                  pl.BlockSpec((1, HEAD_DIM), lambda i, j: (0, 0))],
        out_specs=[pl.BlockSpec((tm, tn), lambda i, j: (i, jnp.minimum(j, 2))),
                   pl.BlockSpec((tm, tn), lambda i, j: (i, jnp.maximum(j - 3, 0)))],
        out_shape=[jax.ShapeDtypeStruct((t, 3 * tn), BF16),
                   jax.ShapeDtypeStruct((t, n - 3 * tn), F32)],
        scratch_shapes=[pltpu.VMEM((tm, d), BF16)],
        compiler_params=_params(("arbitrary", "arbitrary")),
        name="inproj",
    )(x2, mod, g1, w_in, b_in, qg, kg)


def _attn_kernel(q_ref, k_ref, v_ref, o_ref, carry_scr, acc_scr, *, tq, scale):
    i = pl.program_id(2)
    q = q_ref[...]
    row = lax.broadcasted_iota(jnp.int32, (tq, tq), 0)
    col = lax.broadcasted_iota(jnp.int32, (tq, tq), 1)
    tri = (row >= col).astype(BF16)
    ones = jnp.ones((tq, HEAD_DIM), BF16)
    reps = tq // HEAD_DIM

    def tile(j, masked):
        start = pl.multiple_of(j * tq, tq)
        k = k_ref[pl.ds(start, tq), :]
        v = v_ref[pl.ds(start, tq), :]
        z = lax.dot_general(q, k, (((1,), (1,)), ((), ())), preferred_element_type=F32) * scale
        lom = -(jnp.maximum(z, 0.0) + jnp.log1p(jnp.exp(-jnp.abs(z))))
        if masked:
            mask = col < row
            lom = jnp.where(mask, lom, 0.0)
        hi = lom.astype(BF16)
        lo = (lom - hi.astype(F32)).astype(BF16)
        incl = _dot(hi, tri) + _dot(lo, tri)
        tot = _dot(hi, ones) + _dot(lo, ones)
        carry = carry_scr[...]
        w = jnp.exp(z + incl + jnp.concatenate([carry] * reps, axis=1))
        if masked:
            w = jnp.where(mask, w, 0.0)
        acc_scr[...] += _dot(w.astype(BF16), v)
        carry = carry + tot
        carry_scr[...] = carry
        return jnp.max(carry)

    carry_scr[...] = jnp.zeros_like(carry_scr)
    acc_scr[...] = jnp.zeros_like(acc_scr)
    mx0 = tile(i, True)

    def cond(s):
        j, mx = s
        return jnp.logical_and(j >= 0, mx > -F32_EXP_UNDERFLOW)

    def body(s):
        j, _ = s
        return j - 1, tile(j, False)

    lax.while_loop(cond, body, (i - 1, mx0))
    o_ref[...] = acc_scr[...].astype(o_ref.dtype)


def _attn(qkv, batch, seq, n_heads, tq=256):
    t = qkv.shape[0]
    nq = seq // tq
    kern = functools.partial(_attn_kernel, tq=tq, scale=1.0 / math.sqrt(HEAD_DIM))
    return pl.pallas_call(
        kern,
        grid=(batch, n_heads, nq),
        in_specs=[pl.BlockSpec((tq, HEAD_DIM), lambda b, h, i: (b * nq + i, h)),
                  pl.BlockSpec((seq, HEAD_DIM), lambda b, h, i: (b, n_heads + h)),
                  pl.BlockSpec((seq, HEAD_DIM), lambda b, h, i: (b, 2 * n_heads + h))],
        out_specs=pl.BlockSpec((tq, HEAD_DIM), lambda b, h, i: (b * nq + i, h)),
        out_shape=jax.ShapeDtypeStruct((t, n_heads * HEAD_DIM), BF16),
        scratch_shapes=[pltpu.VMEM((tq, HEAD_DIM), F32), pltpu.VMEM((tq, HEAD_DIM), F32)],
        compiler_params=_params(("arbitrary", "arbitrary", "arbitrary")),
        name="attn",
    )(qkv, qkv, qkv)


def _lru_kernel(x_ref, gate_ref, cw_ref, cb_ref, wg_ref, bg_ref, lam_ref, o_ref,
                xpad, a_scr, b_scr, h_scr, *, tt, conv_w, n_blocks, blk):
    s = pl.program_id(1)

    @pl.when(s == 0)
    def _():
        xpad[0:SUBLANES, :] = jnp.zeros((SUBLANES, xpad.shape[1]), F32)
        h_scr[...] = jnp.zeros_like(h_scr)

    xpad[SUBLANES:SUBLANES + tt, :] = x_ref[...]
    u = cb_ref[...]
    for tap in range(conv_w):
        off = SUBLANES - (conv_w - 1) + tap
        u = u + cw_ref[tap:tap + 1, :] * xpad[off:off + tt, :]
    xpad[0:SUBLANES, :] = xpad[tt:tt + SUBLANES, :]

    log_sig_lam = jax.nn.log_sigmoid(lam_ref[...])
    for n in range(n_blocks):
        sl = slice(n * blk, (n + 1) * blk)
        un = u[:, sl]
        g = _dot(un.astype(BF16), wg_ref[n]) + bg_ref[n]
        r = jax.nn.sigmoid(g[:, :blk])
        ig = jax.nn.sigmoid(g[:, blk:])
        log_a = LRU_C * r * log_sig_lam[:, sl]
        a = jnp.exp(log_a)
        a_scr[:, sl] = a
        b_scr[:, sl] = jnp.sqrt(-jnp.tanh(log_a) * (a * a + 1.0)) * (ig * un)

    a = a_scr[...]
    b = b_scr[...]
    rid = lax.broadcasted_iota(jnp.int32, a.shape, 0) % SUBLANES
    for k in (1, 2, 4):
        m = rid >= k
        b = jnp.where(m, a * pltpu.roll(b, k, axis=0) + b, b)
        a = jnp.where(m, a * pltpu.roll(a, k, axis=0), a)
    a_scr[...] = a
    b_scr[...] = b

    def body(g, h):
        rows = pl.ds(pl.multiple_of(g * SUBLANES, SUBLANES), SUBLANES)
        h8 = b_scr[rows, :] + a_scr[rows, :] * h
        b_scr[rows, :] = h8
        return jnp.broadcast_to(h8[SUBLANES - 1:SUBLANES, :], h8.shape)

    h_scr[...] = lax.fori_loop(0, tt // SUBLANES, body, h_scr[...])
    o_ref[...] = (b_scr[...] * jax.nn.gelu(gate_ref[...])).astype(o_ref.dtype)


def _lru(rest, conv_w, conv_b, wg, bg, lam, batch, seq, tt=512):
    t = rest.shape[0]
    k, w = conv_w.shape
    n_blocks, blk = wg.shape[0], wg.shape[1]
    ns = seq // tt
    kern = functools.partial(_lru_kernel, tt=tt, conv_w=k, n_blocks=n_blocks, blk=blk)
    return pl.pallas_call(
        kern,
        grid=(batch, ns),
        in_specs=[pl.BlockSpec((tt, w), lambda b, s: (b * ns + s, 0)),
                  pl.BlockSpec((tt, w), lambda b, s: (b * ns + s, 1)),
                  pl.BlockSpec((k, w), lambda b, s: (0, 0)),
                  pl.BlockSpec((1, w), lambda b, s: (0, 0)),
                  pl.BlockSpec((n_blocks, blk, 2 * blk), lambda b, s: (0, 0, 0)),
                  pl.BlockSpec((n_blocks, 1, 2 * blk), lambda b, s: (0, 0, 0)),
                  pl.BlockSpec((1, w), lambda b, s: (0, 0))],
        out_specs=pl.BlockSpec((tt, w), lambda b, s: (b * ns + s, 0)),
        out_shape=jax.ShapeDtypeStruct((t, w), BF16),
        scratch_shapes=[pltpu.VMEM((tt + SUBLANES, w), F32), pltpu.VMEM((tt, w), F32),
                        pltpu.VMEM((tt, w), F32), pltpu.VMEM((SUBLANES, w), F32)],
        compiler_params=_params(("arbitrary", "arbitrary")),
        name="lru",
    )(rest, rest, conv_w, conv_b, wg, bg, lam)


def _merge_kernel(att_ref, yb_ref, ga_ref, gb_ref, x_ref, mod_ref, wa_ref, wb_ref, wo_ref, g2_ref,
                  x1_ref, h2_ref):
    ya = _dot(att_ref[...], wa_ref[...])
    yb = _dot(yb_ref[...], wb_ref[...])
    merged = jax.nn.sigmoid(ga_ref[...]) * ya + jax.nn.sigmoid(gb_ref[...]) * yb
    x1 = x_ref[...] + mod_ref[2:3, :] * _dot(merged.astype(BF16), wo_ref[...])
    x1_ref[...] = x1
    ms = jnp.mean(x1 * x1, axis=-1, keepdims=True)
    y = x1 * lax.rsqrt(ms + NORM_EPS) * g2_ref[...]
    h2_ref[...] = (y * (1.0 + mod_ref[4:5, :]) + mod_ref[3:4, :]).astype(BF16)


def _merge(att, yb, rest, x2, mod, wa, wb, wo, g2, seq, tm=256):
    t, d = x2.shape
    wa_k, wb_k = wa.shape[0], wb.shape[0]
    tiles_per_batch = seq // tm
    const = lambda shape: pl.BlockSpec(shape, lambda i: (0,) * len(shape), pipeline_mode=pl.Buffered(1))
    return pl.pallas_call(
        _merge_kernel,
        grid=(t // tm,),
        in_specs=[pl.BlockSpec((tm, wa_k), lambda i: (i, 0)),
                  pl.BlockSpec((tm, wb_k), lambda i: (i, 0)),
                  pl.BlockSpec((tm, d), lambda i: (i, 1)),
                  pl.BlockSpec((tm, d), lambda i: (i, 2)),
                  pl.BlockSpec((tm, d), lambda i: (i, 0)),
                  pl.BlockSpec((None, N_MOD, d), lambda i: (i // tiles_per_batch, 0, 0)),
                  const((wa_k, d)), const((wb_k, d)), const((d, d)), const((1, d))],
        out_specs=[pl.BlockSpec((tm, d), lambda i: (i, 0)),
                   pl.BlockSpec((tm, d), lambda i: (i, 0))],
        out_shape=[jax.ShapeDtypeStruct((t, d), F32), jax.ShapeDtypeStruct((t, d), BF16)],
        compiler_params=_params(("arbitrary",)),
        name="merge",
    )(att, yb, rest, rest, x2, mod, wa, wb, wo, g2)


def _mlp_kernel(h2_ref, x1_ref, mod_ref, w1_ref, w2_ref, o_ref, acc_scr):
    f = pl.program_id(1)

    @pl.when(f == 0)
    def _():
        acc_scr[...] = jnp.zeros_like(acc_scr)

    a = jnp.maximum(_dot(h2_ref[...], w1_ref[...]), 0.0)
    acc_scr[...] += _dot((a * a).astype(BF16), w2_ref[...])

    @pl.when(f == pl.num_programs(1) - 1)
    def _():
        o_ref[...] = x1_ref[...] + mod_ref[5:6, :] * acc_scr[...]


def _mlp(h2, x1, mod, w1, w2, seq, tm=512, tf=1024):
    t, d = x1.shape
    dff = w1.shape[1]
    tiles_per_batch = seq // tm
    return pl.pallas_call(
        _mlp_kernel,
        grid=(t // tm, dff // tf),
        in_specs=[pl.BlockSpec((tm, d), lambda i, f: (i, 0)),
                  pl.BlockSpec((tm, d), lambda i, f: (i, 0)),
                  pl.BlockSpec((None, N_MOD, d), lambda i, f: (i // tiles_per_batch, 0, 0)),
                  pl.BlockSpec((d, tf), lambda i, f: (0, f)),
                  pl.BlockSpec((tf, d), lambda i, f: (f, 0))],
        out_specs=pl.BlockSpec((tm, d), lambda i, f: (i, 0)),
        out_shape=jax.ShapeDtypeStruct((t, d), F32),
        scratch_shapes=[pltpu.VMEM((tm, d), F32)],
        compiler_params=_params(("arbitrary", "arbitrary")),
        name="mlp",
    )(h2, x1, mod, w1, w2)


def kernel(x, c, w_ada, b_ada, norm1_g, w_in, b_in, q_norm_g, k_norm_g, conv_w, conv_b, w_rg, b_rg, w_ig, b_ig,
           lru_lambda, w_branch_a, w_branch_b, w_out, norm2_g, w_mlp_in, w_mlp_out):
    batch, seq, d = x.shape
    depth = w_ada.shape[0]
    n_blocks, blk = w_rg.shape[1], w_rg.shape[2]
    n_heads = w_branch_a.shape[1] // HEAD_DIM
    row = lambda v: v.reshape(1, -1)

    x2 = x.reshape(batch * seq, d)
    c_pad = jnp.zeros((SUBLANES, d), F32).at[:batch].set(c)
    for l in range(depth):
        mod = _ada(c_pad, w_ada[l], row(b_ada[l]))[:batch].reshape(batch, N_MOD, d)
        qkv, rest = _inproj(x2, mod, row(norm1_g[l]), w_in[l].astype(BF16), row(b_in[l]),
                            row(q_norm_g[l]), row(k_norm_g[l]), seq)
        att = _attn(qkv, batch, seq, n_heads)
        wg = jnp.concatenate([w_rg[l], w_ig[l]], axis=-1).astype(BF16)
        bg = jnp.concatenate([b_rg[l].reshape(n_blocks, 1, blk), b_ig[l].reshape(n_blocks, 1, blk)], axis=-1)
        yb = _lru(rest, conv_w[l], row(conv_b[l]), wg, bg, row(lru_lambda[l]), batch, seq)
        x1, h2 = _merge(att, yb, rest, x2, mod, w_branch_a[l].astype(BF16), w_branch_b[l].astype(BF16),
                        w_out[l].astype(BF16), row(norm2_g[l]), seq)
        x2 = _mlp(h2, x1, mod, w_mlp_in[l].astype(BF16), w_mlp_out[l].astype(BF16), seq)
    return x2.reshape(batch, seq, d)
```

```python
import functools
import math

import jax
import jax.numpy as jnp
from jax import lax
from jax.experimental import pallas as pl
from jax.experimental.pallas import tpu as pltpu

F32 = jnp.float32
BF16 = jnp.bfloat16

NORM_EPS = 1e-6
LRU_C = 8.0
HEAD_DIM = 128
N_MOD = 6
SUBLANES = 8
F32_EXP_UNDERFLOW = 104.0
VMEM_LIMIT = 56 * 1024 * 1024


def _params(sem, vmem=VMEM_LIMIT):
    return pltpu.CompilerParams(dimension_semantics=sem, vmem_limit_bytes=vmem)


def _dot(a, b):
    return jnp.dot(a, b, preferred_element_type=F32)


def _ada_kernel(c_ref, w_ref, b_ref, o_ref):
    ca = jax.nn.silu(c_ref[...]).astype(BF16)
    o_ref[...] = _dot(ca, w_ref[...].astype(BF16)) + b_ref[...]


def _ada(c_pad, w_ada, b_ada, tn=1024):
    m, d = c_pad.shape
    n = w_ada.shape[1]
    return pl.pallas_call(
        _ada_kernel,
        grid=(n // tn,),
        in_specs=[pl.BlockSpec((m, d), lambda j: (0, 0)),
                  pl.BlockSpec((d, tn), lambda j: (0, j)),
                  pl.BlockSpec((1, tn), lambda j: (0, j))],
        out_specs=pl.BlockSpec((m, tn), lambda j: (0, j)),
        out_shape=jax.ShapeDtypeStruct((m, n), F32),
        compiler_params=_params(("arbitrary",)),
        name="ada",
    )(c_pad, w_ada, b_ada)


def _inproj_kernel(x_ref, mod_ref, g1_ref, w_ref, b_ref, qg_ref, kg_ref, obf_ref, of_ref, h_scr, *, n_heads):
    j = pl.program_id(1)

    @pl.when(j == 0)
    def _():
        x = x_ref[...]
        ms = jnp.mean(x * x, axis=-1, keepdims=True)
        y = x * lax.rsqrt(ms + NORM_EPS) * g1_ref[...]
        h_scr[...] = (y * (1.0 + mod_ref[1:2, :]) + mod_ref[0:1, :]).astype(BF16)

    acc = _dot(h_scr[...], w_ref[...]) + b_ref[...]

    @pl.when(j < 2)
    def _():
        g = jnp.where(j == 0, qg_ref[...], kg_ref[...])
        for hh in range(n_heads):
            sl = slice(hh * HEAD_DIM, (hh + 1) * HEAD_DIM)
            a = acc[:, sl]
            ms = jnp.mean(a * a, axis=-1, keepdims=True)
            obf_ref[:, sl] = (a * lax.rsqrt(ms + NORM_EPS) * g).astype(BF16)

    @pl.when(j == 2)
    def _():
        obf_ref[...] = acc.astype(BF16)

    @pl.when(j >= 3)
    def _():
        of_ref[...] = acc


def _inproj(x2, mod, g1, w_in, b_in, qg, kg, seq, tm=1024, tn=1024):
    t, d = x2.shape
    n = w_in.shape[1]
    nj = n // tn
    tiles_per_batch = seq // tm
    kern = functools.partial(_inproj_kernel, n_heads=tn // HEAD_DIM)
    return pl.pallas_call(
        kern,
        grid=(t // tm, nj),
        in_specs=[pl.BlockSpec((tm, d), lambda i, j: (i, 0)),
                  pl.BlockSpec((None, N_MOD, d), lambda i, j: (i // tiles_per_batch, 0, 0)),
                  pl.BlockSpec((1, d), lambda i, j: (0, 0)),
                  pl.BlockSpec((d, tn), lambda i, j: (0, j)),
                  pl.BlockSpec((1, tn), lambda i, j: (0, j)),
                  pl.BlockSpec((1, HEAD_DIM), lambda i, j: (0, 0)),
                  pl.BlockSpec((1, HEAD_DIM), lambda i, j: (0, 0))],
        out_specs=[pl.BlockSpec((tm, tn), lambda i, j: (i, jnp.minimum(j, 2))),
                   pl.BlockSpec((tm, tn), lambda i, j: (i, jnp.maximum(j - 3, 0)))],
        out_shape=[jax.ShapeDtypeStruct((t, 3 * tn), BF16),
                   jax.ShapeDtypeStruct((t, n - 3 * tn), F32)],
        scratch_shapes=[pltpu.VMEM((tm, d), BF16)],
        compiler_params=_params(("arbitrary", "arbitrary")),
        name="inproj",
    )(x2, mod, g1, w_in, b_in, qg, kg)


def _cumsum_matrix(width):
    j = jnp.arange(width)
    m = jnp.concatenate([(j[:, None] >= j[None, :]).astype(BF16), jnp.ones((width, HEAD_DIM), BF16)], axis=1)
    return jnp.concatenate([m, m], axis=0)


def _attn_kernel(q_ref, k_ref, v_ref, m_ref, o_ref, carry_scr, acc_scr, *, qb, hb, first_fast, scale):
    i = pl.program_id(2)
    blk = HEAD_DIM
    row = lax.broadcasted_iota(jnp.int32, (blk, blk), 0)
    col = lax.broadcasted_iota(jnp.int32, (blk, blk), 1)
    causal = col < row

    def tiles(jobs, nb, diag, first):
        sl = lambda n: slice(n * blk, (n + 1) * blk)
        zs = [lax.dot_general(q_ref[sl(s), sl(h)], k_ref[pl.ds(kstart, nb * blk), sl(h)],
                              (((1,), (1,)), ((), ())), preferred_element_type=F32) * scale
              for s, h, kstart in jobs]
        lhs = []
        for z in zs:
            lom = -(jnp.maximum(z, 0.0) + jnp.log(1.0 + jnp.exp(-jnp.abs(z))))
            row_lhs = []
            for c in range(nb):
                lom_c = jnp.where(causal, lom[:, sl(c)], 0.0) if (diag and c == nb - 1) else lom[:, sl(c)]
                hi = lom_c.astype(BF16)
                lo = (lom_c - hi.astype(F32)).astype(BF16)
                row_lhs.append(jnp.concatenate([hi, lo], axis=1))
            lhs.append(row_lhs)
        boths = [[_dot(a, m_ref[...]) for a in row_lhs] for row_lhs in lhs]
        ws = []
        for (s, h, _), z, both in zip(jobs, zs, boths):
            carry = None if first else carry_scr[s, h]
            w = [None] * nb
            for c in reversed(range(nb)):
                logw = z[:, sl(c)] + both[c][:, :blk]
                if carry is not None:
                    logw = logw + carry
                e = jnp.exp(logw)
                w[c] = (jnp.where(causal, e, 0.0) if (diag and c == nb - 1) else e).astype(BF16)
                carry = both[c][:, blk:] if carry is None else carry + both[c][:, blk:]
            carry_scr[s, h] = carry
            ws.append(w[0] if nb == 1 else jnp.concatenate(w, axis=1))
        for (s, h, kstart), w in zip(jobs, ws):
            pv = _dot(w, v_ref[pl.ds(kstart, nb * blk), sl(h)])
            acc_scr[s, h] = pv if first else acc_scr[s, h] + pv

    g0 = i * qb
    fast = i >= first_fast

    def jobs_for(s, block):
        kstart = pl.multiple_of(block * blk, blk)
        return [(s, h, kstart) for h in range(hb)]

    @pl.when(fast)
    def _():
        tiles([job for s in range(qb) for job in jobs_for(s, g0 + s - 2)], 3, True, True)

    @pl.when(jnp.logical_not(fast))
    def _():
        tiles([job for s in range(qb) for job in jobs_for(s, g0 + s)], 1, True, True)

    jbase = g0 - jnp.where(fast, 3, 1)
    nmax = jbase + qb - 1

    def cond(st):
        n, mx = st
        return jnp.logical_and(n <= nmax, mx > -F32_EXP_UNDERFLOW)

    def body(st):
        n, _ = st
        for s in range(qb):
            j = jbase + s - n

            @pl.when(j >= 0)
            def _():
                tiles(jobs_for(s, j), 1, False, False)

        return n + 1, jnp.max(carry_scr[...])

    lax.while_loop(cond, body, (jnp.int32(0), jnp.max(carry_scr[...])))
    for s in range(qb):
        for h in range(hb):
            o_ref[s * blk:(s + 1) * blk, h * blk:(h + 1) * blk] = acc_scr[s, h].astype(o_ref.dtype)


def _attn(qkv, batch, seq, n_heads, qb=2, hb=4):
    t = qkv.shape[0]
    blk = HEAD_DIM
    tq = qb * blk
    nq = seq // tq
    ng = n_heads // hb
    first_fast = -(-2 // qb)
    kern = functools.partial(_attn_kernel, qb=qb, hb=hb, first_fast=first_fast, scale=1.0 / math.sqrt(HEAD_DIM))
    once = pl.Buffered(1)
    m = _cumsum_matrix(blk)
    return pl.pallas_call(
        kern,
        grid=(batch, ng, nq),
        in_specs=[pl.BlockSpec((tq, hb * blk), lambda b, g, i: (b * nq + i, g)),
                  pl.BlockSpec((seq, hb * blk), lambda b, g, i: (b, ng + g), pipeline_mode=once),
                  pl.BlockSpec((seq, hb * blk), lambda b, g, i: (b, 2 * ng + g), pipeline_mode=once),
                  pl.BlockSpec(m.shape, lambda b, g, i: (0, 0), pipeline_mode=once)],
        out_specs=pl.BlockSpec((tq, hb * blk), lambda b, g, i: (b * nq + i, g)),
        out_shape=jax.ShapeDtypeStruct((t, n_heads * blk), BF16),
        scratch_shapes=[pltpu.VMEM((qb, hb, blk, blk), F32), pltpu.VMEM((qb, hb, blk, blk), F32)],
        compiler_params=_params(("arbitrary", "arbitrary", "arbitrary")),
        name="attn",
    )(qkv, qkv, qkv, m)


def _lru_kernel(x_ref, gate_ref, cw_ref, cb_ref, wg_ref, bg_ref, lam_ref, o_ref,
                xpad, a_scr, b_scr, h_scr, *, tt, conv_w, n_blocks, blk):
    s = pl.program_id(1)

    @pl.when(s == 0)
    def _():
        xpad[0:SUBLANES, :] = jnp.zeros((SUBLANES, xpad.shape[1]), F32)
        h_scr[...] = jnp.zeros_like(h_scr)

    xpad[SUBLANES:SUBLANES + tt, :] = x_ref[...]
    u = cb_ref[...]
    for tap in range(conv_w):
        off = SUBLANES - (conv_w - 1) + tap
        u = u + cw_ref[tap:tap + 1, :] * xpad[off:off + tt, :]
    xpad[0:SUBLANES, :] = xpad[tt:tt + SUBLANES, :]

    log_sig_lam = jax.nn.log_sigmoid(lam_ref[...])
    for n in range(n_blocks):
        sl = slice(n * blk, (n + 1) * blk)
        un = u[:, sl]
        g = _dot(un.astype(BF16), wg_ref[n]) + bg_ref[n]
        r = jax.nn.sigmoid(g[:, :blk])
        ig = jax.nn.sigmoid(g[:, blk:])
        log_a = LRU_C * r * log_sig_lam[:, sl]
        a = jnp.exp(log_a)
        a_scr[:, sl] = a
        b_scr[:, sl] = jnp.sqrt(-jnp.tanh(log_a) * (a * a + 1.0)) * (ig * un)

    a = a_scr[...]
    b = b_scr[...]
    rid = lax.broadcasted_iota(jnp.int32, a.shape, 0) % SUBLANES
    for k in (1, 2, 4):
        m = rid >= k
        b = jnp.where(m, a * pltpu.roll(b, k, axis=0) + b, b)
        a = jnp.where(m, a * pltpu.roll(a, k, axis=0), a)
    a_scr[...] = a
    b_scr[...] = b

    def body(g, h):
        rows = pl.ds(pl.multiple_of(g * SUBLANES, SUBLANES), SUBLANES)
        h8 = b_scr[rows, :] + a_scr[rows, :] * h
        b_scr[rows, :] = h8
        return jnp.broadcast_to(h8[SUBLANES - 1:SUBLANES, :], h8.shape)

    h_scr[...] = lax.fori_loop(0, tt // SUBLANES, body, h_scr[...])
    o_ref[...] = (b_scr[...] * jax.nn.gelu(gate_ref[...])).astype(o_ref.dtype)


def _lru(rest, conv_w, conv_b, wg, bg, lam, batch, seq, tt=512):
    t = rest.shape[0]
    k, w = conv_w.shape
    n_blocks, blk = wg.shape[0], wg.shape[1]
    ns = seq // tt
    kern = functools.partial(_lru_kernel, tt=tt, conv_w=k, n_blocks=n_blocks, blk=blk)
    return pl.pallas_call(
        kern,
        grid=(batch, ns),
        in_specs=[pl.BlockSpec((tt, w), lambda b, s: (b * ns + s, 0)),
                  pl.BlockSpec((tt, w), lambda b, s: (b * ns + s, 1)),
                  pl.BlockSpec((k, w), lambda b, s: (0, 0)),
                  pl.BlockSpec((1, w), lambda b, s: (0, 0)),
                  pl.BlockSpec((n_blocks, blk, 2 * blk), lambda b, s: (0, 0, 0)),
                  pl.BlockSpec((n_blocks, 1, 2 * blk), lambda b, s: (0, 0, 0)),
                  pl.BlockSpec((1, w), lambda b, s: (0, 0))],
        out_specs=pl.BlockSpec((tt, w), lambda b, s: (b * ns + s, 0)),
        out_shape=jax.ShapeDtypeStruct((t, w), BF16),
        scratch_shapes=[pltpu.VMEM((tt + SUBLANES, w), F32), pltpu.VMEM((tt, w), F32),
                        pltpu.VMEM((tt, w), F32), pltpu.VMEM((SUBLANES, w), F32)],
        compiler_params=_params(("arbitrary", "arbitrary")),
        name="lru",
    )(rest, rest, conv_w, conv_b, wg, bg, lam)


def _merge_kernel(att_ref, yb_ref, ga_ref, gb_ref, x_ref, mod_ref, wa_ref, wb_ref, wo_ref, g2_ref,
                  x1_ref, h2_ref):
    ya = _dot(att_ref[...], wa_ref[...])
    yb = _dot(yb_ref[...], wb_ref[...])
    merged = jax.nn.sigmoid(ga_ref[...]) * ya + jax.nn.sigmoid(gb_ref[...]) * yb
    x1 = x_ref[...] + mod_ref[2:3, :] * _dot(merged.astype(BF16), wo_ref[...])
    x1_ref[...] = x1
    ms = jnp.mean(x1 * x1, axis=-1, keepdims=True)
    y = x1 * lax.rsqrt(ms + NORM_EPS) * g2_ref[...]
    h2_ref[...] = (y * (1.0 + mod_ref[4:5, :]) + mod_ref[3:4, :]).astype(BF16)


def _merge(att, yb, rest, x2, mod, wa, wb, wo, g2, seq, tm=256):
    t, d = x2.shape
    wa_k, wb_k = wa.shape[0], wb.shape[0]
    tiles_per_batch = seq // tm
    const = lambda shape: pl.BlockSpec(shape, lambda i: (0,) * len(shape), pipeline_mode=pl.Buffered(1))
    return pl.pallas_call(
        _merge_kernel,
        grid=(t // tm,),
        in_specs=[pl.BlockSpec((tm, wa_k), lambda i: (i, 0)),
                  pl.BlockSpec((tm, wb_k), lambda i: (i, 0)),
                  pl.BlockSpec((tm, d), lambda i: (i, 1)),
                  pl.BlockSpec((tm, d), lambda i: (i, 2)),
                  pl.BlockSpec((tm, d), lambda i: (i, 0)),
                  pl.BlockSpec((None, N_MOD, d), lambda i: (i // tiles_per_batch, 0, 0)),
                  const((wa_k, d)), const((wb_k, d)), const((d, d)), const((1, d))],
        out_specs=[pl.BlockSpec((tm, d), lambda i: (i, 0)),
                   pl.BlockSpec((tm, d), lambda i: (i, 0))],
        out_shape=[jax.ShapeDtypeStruct((t, d), F32), jax.ShapeDtypeStruct((t, d), BF16)],
        compiler_params=_params(("arbitrary",)),
        name="merge",
    )(att, yb, rest, rest, x2, mod, wa, wb, wo, g2)


def _mlp_kernel(h2_ref, x1_ref, mod_ref, w1_ref, w2_ref, o_ref, acc_scr):
    f = pl.program_id(1)

    @pl.when(f == 0)
    def _():
        acc_scr[...] = jnp.zeros_like(acc_scr)

    a = jnp.maximum(_dot(h2_ref[...], w1_ref[...]), 0.0)
    acc_scr[...] += _dot((a * a).astype(BF16), w2_ref[...])

    @pl.when(f == pl.num_programs(1) - 1)
    def _():
        o_ref[...] = x1_ref[...] + mod_ref[5:6, :] * acc_scr[...]


def _mlp(h2, x1, mod, w1, w2, seq, tm=512, tf=1024):
    t, d = x1.shape
    dff = w1.shape[1]
    tiles_per_batch = seq // tm
    return pl.pallas_call(
        _mlp_kernel,
        grid=(t // tm, dff // tf),
        in_specs=[pl.BlockSpec((tm, d), lambda i, f: (i, 0)),
                  pl.BlockSpec((tm, d), lambda i, f: (i, 0)),
                  pl.BlockSpec((None, N_MOD, d), lambda i, f: (i // tiles_per_batch, 0, 0)),
                  pl.BlockSpec((d, tf), lambda i, f: (0, f)),
                  pl.BlockSpec((tf, d), lambda i, f: (f, 0))],
        out_specs=pl.BlockSpec((tm, d), lambda i, f: (i, 0)),
        out_shape=jax.ShapeDtypeStruct((t, d), F32),
        scratch_shapes=[pltpu.VMEM((tm, d), F32)],
        compiler_params=_params(("arbitrary", "arbitrary")),
        name="mlp",
    )(h2, x1, mod, w1, w2)


def kernel(x, c, w_ada, b_ada, norm1_g, w_in, b_in, q_norm_g, k_norm_g, conv_w, conv_b, w_rg, b_rg, w_ig, b_ig,
           lru_lambda, w_branch_a, w_branch_b, w_out, norm2_g, w_mlp_in, w_mlp_out):
    batch, seq, d = x.shape
    depth = w_ada.shape[0]
    n_blocks, blk = w_rg.shape[1], w_rg.shape[2]
    n_heads = w_branch_a.shape[1] // HEAD_DIM
    row = lambda v: v.reshape(1, -1)

    x2 = x.reshape(batch * seq, d)
    c_pad = jnp.zeros((SUBLANES, d), F32).at[:batch].set(c)
    for l in range(depth):
        mod = _ada(c_pad, w_ada[l], row(b_ada[l]))[:batch].reshape(batch, N_MOD, d)
        qkv, rest = _inproj(x2, mod, row(norm1_g[l]), w_in[l].astype(BF16), row(b_in[l]),
                            row(q_norm_g[l]), row(k_norm_g[l]), seq)
        att = _attn(qkv, batch, seq, n_heads)
        wg = jnp.concatenate([w_rg[l], w_ig[l]], axis=-1).astype(BF16)
        bg = jnp.concatenate([b_rg[l].reshape(n_blocks, 1, blk), b_ig[l].reshape(n_blocks, 1, blk)], axis=-1)
        yb = _lru(rest, conv_w[l], row(conv_b[l]), wg, bg, row(lru_lambda[l]), batch, seq)
        x1, h2 = _merge(att, yb, rest, x2, mod, w_branch_a[l].astype(BF16), w_branch_b[l].astype(BF16),
                        w_out[l].astype(BF16), row(norm2_g[l]), seq)
        x2 = _mlp(h2, x1, mod, w_mlp_in[l].astype(BF16), w_mlp_out[l].astype(BF16), seq)
    return x2.reshape(batch, seq, d)
```

```python
import functools
import math

import jax
import jax.numpy as jnp
from jax import lax
from jax.experimental import pallas as pl
from jax.experimental.pallas import tpu as pltpu

F32 = jnp.float32
BF16 = jnp.bfloat16

NORM_EPS = 1e-6
LRU_C = 8.0
HEAD_DIM = 128
N_MOD = 6
SUBLANES = 8
F32_EXP_UNDERFLOW = 104.0
VMEM_LIMIT = 56 * 1024 * 1024


def _params(sem, vmem=VMEM_LIMIT):
    return pltpu.CompilerParams(dimension_semantics=sem, vmem_limit_bytes=vmem)


def _dot(a, b):
    return jnp.dot(a, b, preferred_element_type=F32)


def _ada_kernel(c_ref, w_ref, b_ref, o_ref):
    ca = jax.nn.silu(c_ref[...]).astype(BF16)
    o_ref[...] = _dot(ca, w_ref[...].astype(BF16)) + b_ref[...]


def _ada(c_pad, w_ada, b_ada, tn=1024):
    m, d = c_pad.shape
    n = w_ada.shape[1]
    return pl.pallas_call(
        _ada_kernel,
        grid=(n // tn,),
        in_specs=[pl.BlockSpec((m, d), lambda j: (0, 0)),
                  pl.BlockSpec((d, tn), lambda j: (0, j)),
                  pl.BlockSpec((1, tn), lambda j: (0, j))],
        out_specs=pl.BlockSpec((m, tn), lambda j: (0, j)),
        out_shape=jax.ShapeDtypeStruct((m, n), F32),
        compiler_params=_params(("arbitrary",)),
        name="ada",
    )(c_pad, w_ada, b_ada)


def _inproj_kernel(x_ref, mod_ref, g1_ref, w_ref, b_ref, qg_ref, kg_ref, obf_ref, of_ref, h_scr, *, n_heads, nf):
    j = pl.program_id(1)

    @pl.when(j == 0)
    def _():
        x = x_ref[...]
        ms = jnp.mean(x * x, axis=-1, keepdims=True)
        y = x * lax.rsqrt(ms + NORM_EPS) * g1_ref[...]
        h_scr[...] = (y * (1.0 + mod_ref[1:2, :]) + mod_ref[0:1, :]).astype(BF16)

    of_ref[...] = _dot(h_scr[...], w_ref[...]) + b_ref[...]

    @pl.when(jnp.logical_and(j >= nf, j < nf + 2))
    def _():
        g = jnp.where(j == nf, qg_ref[...], kg_ref[...])
        for hh in range(n_heads):
            sl = slice(hh * HEAD_DIM, (hh + 1) * HEAD_DIM)
            a = of_ref[:, sl]
            ms = jnp.mean(a * a, axis=-1, keepdims=True)
            obf_ref[:, sl] = (a * lax.rsqrt(ms + NORM_EPS) * g).astype(BF16)

    @pl.when(j == nf + 2)
    def _():
        obf_ref[...] = of_ref[...].astype(BF16)


def _inproj(x2, mod, g1, w_in, b_in, qg, kg, seq, tm=1024, tn=1024):
    t, d = x2.shape
    n = w_in.shape[1]
    nj = n // tn
    nf = nj - 3
    tiles_per_batch = seq // tm
    kern = functools.partial(_inproj_kernel, n_heads=tn // HEAD_DIM, nf=nf)
    wcol = lambda j: (j + 3) % nj
    return pl.pallas_call(
        kern,
        grid=(t // tm, nj),
        in_specs=[pl.BlockSpec((tm, d), lambda i, j: (i, 0)),
                  pl.BlockSpec((None, N_MOD, d), lambda i, j: (i // tiles_per_batch, 0, 0)),
                  pl.BlockSpec((1, d), lambda i, j: (0, 0)),
                  pl.BlockSpec((d, tn), lambda i, j: (0, wcol(j))),
                  pl.BlockSpec((1, tn), lambda i, j: (0, wcol(j))),
                  pl.BlockSpec((1, HEAD_DIM), lambda i, j: (0, 0)),
                  pl.BlockSpec((1, HEAD_DIM), lambda i, j: (0, 0))],
        out_specs=[pl.BlockSpec((tm, tn), lambda i, j: (i, jnp.maximum(j - nf, 0))),
                   pl.BlockSpec((tm, tn), lambda i, j: (i, jnp.minimum(j, nf)))],
        out_shape=[jax.ShapeDtypeStruct((t, 3 * tn), BF16),
                   jax.ShapeDtypeStruct((t, (nf + 1) * tn), F32)],
        scratch_shapes=[pltpu.VMEM((tm, d), BF16)],
        compiler_params=_params(("arbitrary", "arbitrary")),
        name="inproj",
    )(x2, mod, g1, w_in, b_in, qg, kg)


def _cumsum_matrix(width):
    j = jnp.arange(width)
    m = jnp.concatenate([(j[:, None] >= j[None, :]).astype(BF16), jnp.ones((width, HEAD_DIM), BF16)], axis=1)
    return jnp.concatenate([m, m], axis=0)


def _attn_kernel(q_ref, k_ref, v_ref, m_ref, o_ref, carry_scr, acc_scr, *, qb, hb, first_fast, scale):
    i = pl.program_id(2)
    blk = HEAD_DIM
    row = lax.broadcasted_iota(jnp.int32, (blk, blk), 0)
    col = lax.broadcasted_iota(jnp.int32, (blk, blk), 1)
    causal = col < row

    def tiles(jobs, nb, diag, first):
        sl = lambda n: slice(n * blk, (n + 1) * blk)
        zs = [lax.dot_general(q_ref[sl(s), sl(h)], k_ref[pl.ds(kstart, nb * blk), sl(h)],
                              (((1,), (1,)), ((), ())), preferred_element_type=F32) * scale
              for s, h, kstart in jobs]
        lhs = []
        for z in zs:
            lom = -(jnp.maximum(z, 0.0) + jnp.log(1.0 + jnp.exp(-jnp.abs(z))))
            row_lhs = []
            for c in range(nb):
                lom_c = jnp.where(causal, lom[:, sl(c)], 0.0) if (diag and c == nb - 1) else lom[:, sl(c)]
                hi = lom_c.astype(BF16)
                lo = (lom_c - hi.astype(F32)).astype(BF16)
                row_lhs.append(jnp.concatenate([hi, lo], axis=1))
            lhs.append(row_lhs)
        boths = [[_dot(a, m_ref[...]) for a in row_lhs] for row_lhs in lhs]
        ws = []
        for (s, h, _), z, both in zip(jobs, zs, boths):
            carry = None if first else carry_scr[s, h]
            w = [None] * nb
            for c in reversed(range(nb)):
                logw = z[:, sl(c)] + both[c][:, :blk]
                if carry is not None:
                    logw = logw + carry
                e = jnp.exp(logw)
                w[c] = (jnp.where(causal, e, 0.0) if (diag and c == nb - 1) else e).astype(BF16)
                carry = both[c][:, blk:] if carry is None else carry + both[c][:, blk:]
            carry_scr[s, h] = carry
            ws.append(w[0] if nb == 1 else jnp.concatenate(w, axis=1))
        for (s, h, kstart), w in zip(jobs, ws):
            pv = _dot(w, v_ref[pl.ds(kstart, nb * blk), sl(h)])
            acc_scr[s, h] = pv if first else acc_scr[s, h] + pv

    g0 = i * qb
    fast = i >= first_fast

    def jobs_for(s, block):
        kstart = pl.multiple_of(block * blk, blk)
        return [(s, h, kstart) for h in range(hb)]

    @pl.when(fast)
    def _():
        tiles([job for s in range(qb) for job in jobs_for(s, g0 + s - 2)], 3, True, True)

    @pl.when(jnp.logical_not(fast))
    def _():
        tiles([job for s in range(qb) for job in jobs_for(s, g0 + s)], 1, True, True)

    jbase = g0 - jnp.where(fast, 3, 1)
    nmax = jbase + qb - 1

    def cond(st):
        n, mx = st
        return jnp.logical_and(n <= nmax, mx > -F32_EXP_UNDERFLOW)

    def body(st):
        n, _ = st
        for s in range(qb):
            j = jbase + s - n

            @pl.when(j >= 0)
            def _():
                tiles(jobs_for(s, j), 1, False, False)

        return n + 1, jnp.max(carry_scr[...])

    lax.while_loop(cond, body, (jnp.int32(0), jnp.max(carry_scr[...])))
    for s in range(qb):
        for h in range(hb):
            o_ref[s * blk:(s + 1) * blk, h * blk:(h + 1) * blk] = acc_scr[s, h].astype(o_ref.dtype)


def _attn(qkv, batch, seq, n_heads, qb=2, hb=4):
    t = qkv.shape[0]
    blk = HEAD_DIM
    tq = qb * blk
    nq = seq // tq
    ng = n_heads // hb
    first_fast = -(-2 // qb)
    kern = functools.partial(_attn_kernel, qb=qb, hb=hb, first_fast=first_fast, scale=1.0 / math.sqrt(HEAD_DIM))
    once = pl.Buffered(1)
    m = _cumsum_matrix(blk)
    return pl.pallas_call(
        kern,
        grid=(batch, ng, nq),
        in_specs=[pl.BlockSpec((tq, hb * blk), lambda b, g, i: (b * nq + i, g)),
                  pl.BlockSpec((seq, hb * blk), lambda b, g, i: (b, ng + g), pipeline_mode=once),
                  pl.BlockSpec((seq, hb * blk), lambda b, g, i: (b, 2 * ng + g), pipeline_mode=once),
                  pl.BlockSpec(m.shape, lambda b, g, i: (0, 0), pipeline_mode=once)],
        out_specs=pl.BlockSpec((tq, hb * blk), lambda b, g, i: (b * nq + i, g)),
        out_shape=jax.ShapeDtypeStruct((t, n_heads * blk), BF16),
        scratch_shapes=[pltpu.VMEM((qb, hb, blk, blk), F32), pltpu.VMEM((qb, hb, blk, blk), F32)],
        compiler_params=_params(("arbitrary", "arbitrary", "arbitrary")),
        name="attn",
    )(qkv, qkv, qkv, m)


def _sigmoid(x):
    return 0.5 * jnp.tanh(0.5 * x) + 0.5


def _lru_kernel(x_ref, gate_ref, cw_ref, cb_ref, wg_ref, bg_ref, lam_ref, o_ref,
                xs, us, a_s, b_s, h_st, *, nb, tt, conv_w, blk, pitch):
    s = pl.program_id(0)
    hist = SUBLANES
    step_rows = lambda ref, t: ref.at[pl.ds(t, SUBLANES, stride=pitch), :]

    @pl.when(s == 0)
    def _():
        for b in range(nb):
            for n in range(SUBLANES):
                xs[b, n * pitch:n * pitch + hist, :] = jnp.zeros((hist, blk), F32)
        h_st[...] = jnp.zeros_like(h_st)

    for b in range(nb):
        for n in range(SUBLANES):
            xs[b, n * pitch + hist:n * pitch + hist + tt, :] = x_ref[b, :, n * blk:(n + 1) * blk]

    taps = [cw_ref[k] for k in range(conv_w)]
    cb = cb_ref[...]

    def conv_body(t, win):
        new = []
        for b in range(nb):
            xt = step_rows(xs.at[b], hist + t)[...]
            vals = list(win[b]) + [xt]
            u = cb
            for k in range(conv_w):
                u = u + taps[k] * vals[k]
            step_rows(us.at[b], t)[...] = u
            new.append(tuple(vals[1:]))
        return tuple(new)

    win0 = tuple(tuple(step_rows(xs.at[b], hist - (conv_w - 1) + k)[...] for k in range(conv_w - 1))
                 for b in range(nb))
    lax.fori_loop(0, tt, conv_body, win0, unroll=8)
    for b in range(nb):
        for n in range(SUBLANES):
            xs[b, n * pitch:n * pitch + hist, :] = xs[b, n * pitch + tt:n * pitch + tt + hist, :]

    log_sig_lam = jax.nn.log_sigmoid(lam_ref[...])
    for b in range(nb):
        for n in range(SUBLANES):
            rows = slice(n * pitch, n * pitch + tt)
            un = us[b, rows, :]
            g = _dot(un.astype(BF16), wg_ref[n]) + bg_ref[n]
            log_a = LRU_C * _sigmoid(g[:, :blk]) * log_sig_lam[:, n * blk:(n + 1) * blk]
            a = jnp.exp(log_a)
            a_s[b, rows, :] = a
            b_s[b, rows, :] = jnp.sqrt(-jnp.tanh(log_a) * (a * a + 1.0)) * (_sigmoid(g[:, blk:]) * un)

    def scan_body(t, hs):
        new = []
        for b in range(nb):
            h = step_rows(a_s.at[b], t)[...] * hs[b] + step_rows(b_s.at[b], t)[...]
            step_rows(b_s.at[b], t)[...] = h
            new.append(h)
        return tuple(new)

    hs = lax.fori_loop(0, tt, scan_body, tuple(h_st[b] for b in range(nb)), unroll=8)
    for b in range(nb):
        h_st[b] = hs[b]
        for n in range(SUBLANES):
            cols = slice(n * blk, (n + 1) * blk)
            o_ref[b, :, cols] = (b_s[b, n * pitch:n * pitch + tt, :] * jax.nn.gelu(gate_ref[b, :, cols])
                                 ).astype(o_ref.dtype)


def _lru(rest, conv_w, conv_b, wg, bg, lam, batch, seq, tt=512):
    k, w = conv_w.shape
    n_blocks, blk = wg.shape[0], wg.shape[1]
    assert n_blocks == SUBLANES and blk == HEAD_DIM, "one 128-lane channel slab per sublane"
    ns = seq // tt
    pitch = tt + 2 * SUBLANES if (tt // SUBLANES) % 2 else tt + SUBLANES
    rest3 = rest.reshape(batch, seq, rest.shape[1])
    kern = functools.partial(_lru_kernel, nb=batch, tt=tt, conv_w=k, blk=blk, pitch=pitch)
    slab = lambda: pltpu.VMEM((batch, SUBLANES * pitch, blk), F32)
    out = pl.pallas_call(
        kern,
        grid=(ns,),
        in_specs=[pl.BlockSpec((batch, tt, w), lambda s: (0, s, 0)),
                  pl.BlockSpec((batch, tt, w), lambda s: (0, s, 1)),
                  pl.BlockSpec((k, SUBLANES, blk), lambda s: (0, 0, 0)),
                  pl.BlockSpec((SUBLANES, blk), lambda s: (0, 0)),
                  pl.BlockSpec((n_blocks, blk, 2 * blk), lambda s: (0, 0, 0)),
                  pl.BlockSpec((n_blocks, 1, 2 * blk), lambda s: (0, 0, 0)),
                  pl.BlockSpec((1, w), lambda s: (0, 0))],
        out_specs=pl.BlockSpec((batch, tt, w), lambda s: (0, s, 0)),
        out_shape=jax.ShapeDtypeStruct((batch, seq, w), BF16),
        scratch_shapes=[slab(), slab(), slab(), slab(), pltpu.VMEM((batch, SUBLANES, blk), F32)],
        compiler_params=_params(("arbitrary",)),
        name="lru",
    )(rest3, rest3, conv_w.reshape(k, SUBLANES, blk), conv_b.reshape(SUBLANES, blk), wg, bg, lam)
    return out.reshape(batch * seq, w)


def _merge_kernel(att_ref, yb_ref, ga_ref, gb_ref, x_ref, mod_ref, wa_ref, wb_ref, wo_ref, g2_ref,
                  x1_ref, h2_ref):
    ya = _dot(att_ref[...], wa_ref[...])
    yb = _dot(yb_ref[...], wb_ref[...])
    merged = jax.nn.sigmoid(ga_ref[...]) * ya + jax.nn.sigmoid(gb_ref[...]) * yb
    x1 = x_ref[...] + mod_ref[2:3, :] * _dot(merged.astype(BF16), wo_ref[...])
    x1_ref[...] = x1
    ms = jnp.mean(x1 * x1, axis=-1, keepdims=True)
    y = x1 * lax.rsqrt(ms + NORM_EPS) * g2_ref[...]
    h2_ref[...] = (y * (1.0 + mod_ref[4:5, :]) + mod_ref[3:4, :]).astype(BF16)


def _merge(att, yb, rest, x2, mod, wa, wb, wo, g2, seq, tm=256):
    t, d = x2.shape
    wa_k, wb_k = wa.shape[0], wb.shape[0]
    tiles_per_batch = seq // tm
    const = lambda shape: pl.BlockSpec(shape, lambda i: (0,) * len(shape), pipeline_mode=pl.Buffered(1))
    return pl.pallas_call(
        _merge_kernel,
        grid=(t // tm,),
        in_specs=[pl.BlockSpec((tm, wa_k), lambda i: (i, 0)),
                  pl.BlockSpec((tm, wb_k), lambda i: (i, 0)),
                  pl.BlockSpec((tm, d), lambda i: (i, 1)),
                  pl.BlockSpec((tm, d), lambda i: (i, 2)),
                  pl.BlockSpec((tm, d), lambda i: (i, 0)),
                  pl.BlockSpec((None, N_MOD, d), lambda i: (i // tiles_per_batch, 0, 0)),
                  const((wa_k, d)), const((wb_k, d)), const((d, d)), const((1, d))],
        out_specs=[pl.BlockSpec((tm, d), lambda i: (i, 0)),
                   pl.BlockSpec((tm, d), lambda i: (i, 0))],
        out_shape=[jax.ShapeDtypeStruct((t, d), F32), jax.ShapeDtypeStruct((t, d), BF16)],
        compiler_params=_params(("arbitrary",)),
        name="merge",
    )(att, yb, rest, rest, x2, mod, wa, wb, wo, g2)


def _mlp_kernel(h2_ref, x1_ref, mod_ref, w1_ref, w2_ref, o_ref, acc_scr):
    f = pl.program_id(1)

    @pl.when(f == 0)
    def _():
        acc_scr[...] = jnp.zeros_like(acc_scr)

    a = jnp.maximum(_dot(h2_ref[...], w1_ref[...]), 0.0)
    acc_scr[...] += _dot((a * a).astype(BF16), w2_ref[...])

    @pl.when(f == pl.num_programs(1) - 1)
    def _():
        o_ref[...] = x1_ref[...] + mod_ref[5:6, :] * acc_scr[...]


def _mlp(h2, x1, mod, w1, w2, seq, tm=512, tf=1024):
    t, d = x1.shape
    dff = w1.shape[1]
    tiles_per_batch = seq // tm
    return pl.pallas_call(
        _mlp_kernel,
        grid=(t // tm, dff // tf),
        in_specs=[pl.BlockSpec((tm, d), lambda i, f: (i, 0)),
                  pl.BlockSpec((tm, d), lambda i, f: (i, 0)),
                  pl.BlockSpec((None, N_MOD, d), lambda i, f: (i // tiles_per_batch, 0, 0)),
                  pl.BlockSpec((d, tf), lambda i, f: (0, f)),
                  pl.BlockSpec((tf, d), lambda i, f: (f, 0))],
        out_specs=pl.BlockSpec((tm, d), lambda i, f: (i, 0)),
        out_shape=jax.ShapeDtypeStruct((t, d), F32),
        scratch_shapes=[pltpu.VMEM((tm, d), F32)],
        compiler_params=_params(("arbitrary", "arbitrary")),
        name="mlp",
    )(h2, x1, mod, w1, w2)


def kernel(x, c, w_ada, b_ada, norm1_g, w_in, b_in, q_norm_g, k_norm_g, conv_w, conv_b, w_rg, b_rg, w_ig, b_ig,
           lru_lambda, w_branch_a, w_branch_b, w_out, norm2_g, w_mlp_in, w_mlp_out):
    batch, seq, d = x.shape
    depth = w_ada.shape[0]
    n_blocks, blk = w_rg.shape[1], w_rg.shape[2]
    n_heads = w_branch_a.shape[1] // HEAD_DIM
    row = lambda v: v.reshape(1, -1)

    x2 = x.reshape(batch * seq, d)
    c_pad = jnp.zeros((SUBLANES, d), F32).at[:batch].set(c)
    for l in range(depth):
        mod = _ada(c_pad, w_ada[l], row(b_ada[l]))[:batch].reshape(batch, N_MOD, d)
        qkv, rest = _inproj(x2, mod, row(norm1_g[l]), w_in[l].astype(BF16), row(b_in[l]),
                            row(q_norm_g[l]), row(k_norm_g[l]), seq)
        att = _attn(qkv, batch, seq, n_heads)
        wg = jnp.concatenate([w_rg[l], w_ig[l]], axis=-1).astype(BF16)
        bg = jnp.concatenate([b_rg[l].reshape(n_blocks, 1, blk), b_ig[l].reshape(n_blocks, 1, blk)], axis=-1)
        yb = _lru(rest, conv_w[l], row(conv_b[l]), wg, bg, row(lru_lambda[l]), batch, seq)
        x1, h2 = _merge(att, yb, rest, x2, mod, w_branch_a[l].astype(BF16), w_branch_b[l].astype(BF16),
                        w_out[l].astype(BF16), row(norm2_g[l]), seq)
        x2 = _mlp(h2, x1, mod, w_mlp_in[l].astype(BF16), w_mlp_out[l].astype(BF16), seq)
    return x2.reshape(batch, seq, d)
```

```python
import functools
import math

import jax
import jax.numpy as jnp
from jax import lax
from jax.experimental import pallas as pl
from jax.experimental.pallas import tpu as pltpu

F32 = jnp.float32
BF16 = jnp.bfloat16

NORM_EPS = 1e-6
LRU_C = 8.0
HEAD_DIM = 128
N_MOD = 6
SUBLANES = 8
F32_EXP_UNDERFLOW = 104.0
VMEM_LIMIT = 56 * 1024 * 1024


def _params(sem, vmem=VMEM_LIMIT):
    return pltpu.CompilerParams(dimension_semantics=sem, vmem_limit_bytes=vmem)


def _dot(a, b):
    return jnp.dot(a, b, preferred_element_type=F32)


def _ada_kernel(c_ref, w_ref, b_ref, o_ref):
    ca = jax.nn.silu(c_ref[...]).astype(BF16)
    o_ref[...] = _dot(ca, w_ref[...].astype(BF16)) + b_ref[...]


def _ada(c_pad, w_ada, b_ada, tn=1024):
    m, d = c_pad.shape
    n = w_ada.shape[1]
    return pl.pallas_call(
        _ada_kernel,
        grid=(n // tn,),
        in_specs=[pl.BlockSpec((m, d), lambda j: (0, 0)),
                  pl.BlockSpec((d, tn), lambda j: (0, j)),
                  pl.BlockSpec((1, tn), lambda j: (0, j))],
        out_specs=pl.BlockSpec((m, tn), lambda j: (0, j)),
        out_shape=jax.ShapeDtypeStruct((m, n), F32),
        compiler_params=_params(("arbitrary",)),
        name="ada",
    )(c_pad, w_ada, b_ada)


def _inproj_kernel(x_ref, mod_ref, g1_ref, w_ref, b_ref, qg_ref, kg_ref, obf_ref, of_ref, h_scr, *, n_heads, nf):
    j = pl.program_id(1)

    @pl.when(j == 0)
    def _():
        x = x_ref[...]
        ms = jnp.mean(x * x, axis=-1, keepdims=True)
        y = x * lax.rsqrt(ms + NORM_EPS) * g1_ref[...]
        h_scr[...] = (y * (1.0 + mod_ref[1:2, :]) + mod_ref[0:1, :]).astype(BF16)

    of_ref[...] = _dot(h_scr[...], w_ref[...]) + b_ref[...]

    @pl.when(jnp.logical_and(j >= nf, j < nf + 2))
    def _():
        g = jnp.where(j == nf, qg_ref[...], kg_ref[...])
        for hh in range(n_heads):
            sl = slice(hh * HEAD_DIM, (hh + 1) * HEAD_DIM)
            a = of_ref[:, sl]
            ms = jnp.mean(a * a, axis=-1, keepdims=True)
            obf_ref[:, sl] = (a * lax.rsqrt(ms + NORM_EPS) * g).astype(BF16)

    @pl.when(j == nf + 2)
    def _():
        obf_ref[...] = of_ref[...].astype(BF16)


def _inproj(x2, mod, g1, w_in, b_in, qg, kg, seq, tm=1024, tn=1024):
    t, d = x2.shape
    n = w_in.shape[1]
    nj = n // tn
    nf = nj - 3
    tiles_per_batch = seq // tm
    kern = functools.partial(_inproj_kernel, n_heads=tn // HEAD_DIM, nf=nf)
    wcol = lambda j: (j + 3) % nj
    return pl.pallas_call(
        kern,
        grid=(t // tm, nj),
        in_specs=[pl.BlockSpec((tm, d), lambda i, j: (i, 0)),
                  pl.BlockSpec((None, N_MOD, d), lambda i, j: (i // tiles_per_batch, 0, 0)),
                  pl.BlockSpec((1, d), lambda i, j: (0, 0)),
                  pl.BlockSpec((d, tn), lambda i, j: (0, wcol(j))),
                  pl.BlockSpec((1, tn), lambda i, j: (0, wcol(j))),
                  pl.BlockSpec((1, HEAD_DIM), lambda i, j: (0, 0)),
                  pl.BlockSpec((1, HEAD_DIM), lambda i, j: (0, 0))],
        out_specs=[pl.BlockSpec((tm, tn), lambda i, j: (i, jnp.maximum(j - nf, 0))),
                   pl.BlockSpec((tm, tn), lambda i, j: (i, jnp.minimum(j, nf)))],
        out_shape=[jax.ShapeDtypeStruct((t, 3 * tn), BF16),
                   jax.ShapeDtypeStruct((t, (nf + 1) * tn), F32)],
        scratch_shapes=[pltpu.VMEM((tm, d), BF16)],
        compiler_params=_params(("arbitrary", "arbitrary")),
        name="inproj",
    )(x2, mod, g1, w_in, b_in, qg, kg)


def _cumsum_matrix(width):
    j = jnp.arange(width)
    m = jnp.concatenate([(j[:, None] >= j[None, :]).astype(BF16), jnp.ones((width, HEAD_DIM), BF16)], axis=1)
    return jnp.concatenate([m, m], axis=0)


def _attn_kernel(q_ref, k_ref, v_ref, m_ref, o_ref, carry_scr, acc_scr, *, qb, hb, first_fast, scale):
    i = pl.program_id(2)
    blk = HEAD_DIM
    row = lax.broadcasted_iota(jnp.int32, (blk, blk), 0)
    col = lax.broadcasted_iota(jnp.int32, (blk, blk), 1)
    causal = col < row

    def tiles(jobs, nb, diag, first):
        sl = lambda n: slice(n * blk, (n + 1) * blk)
        zs = [lax.dot_general(q_ref[sl(s), sl(h)], k_ref[pl.ds(kstart, nb * blk), sl(h)],
                              (((1,), (1,)), ((), ())), preferred_element_type=F32) * scale
              for s, h, kstart in jobs]
        lhs = []
        for z in zs:
            lom = -(jnp.maximum(z, 0.0) + jnp.log(1.0 + jnp.exp(-jnp.abs(z))))
            row_lhs = []
            for c in range(nb):
                lom_c = jnp.where(causal, lom[:, sl(c)], 0.0) if (diag and c == nb - 1) else lom[:, sl(c)]
                hi = lom_c.astype(BF16)
                lo = (lom_c - hi.astype(F32)).astype(BF16)
                row_lhs.append(jnp.concatenate([hi, lo], axis=1))
            lhs.append(row_lhs)
        boths = [[_dot(a, m_ref[...]) for a in row_lhs] for row_lhs in lhs]
        ws = []
        for (s, h, _), z, both in zip(jobs, zs, boths):
            carry = None if first else carry_scr[s, h]
            w = [None] * nb
            for c in reversed(range(nb)):
                logw = z[:, sl(c)] + both[c][:, :blk]
                if carry is not None:
                    logw = logw + carry
                e = jnp.exp(logw)
                w[c] = (jnp.where(causal, e, 0.0) if (diag and c == nb - 1) else e).astype(BF16)
                carry = both[c][:, blk:] if carry is None else carry + both[c][:, blk:]
            carry_scr[s, h] = carry
            ws.append(w[0] if nb == 1 else jnp.concatenate(w, axis=1))
        for (s, h, kstart), w in zip(jobs, ws):
            pv = _dot(w, v_ref[pl.ds(kstart, nb * blk), sl(h)])
            acc_scr[s, h] = pv if first else acc_scr[s, h] + pv

    g0 = i * qb
    fast = i >= first_fast

    def jobs_for(s, block):
        kstart = pl.multiple_of(block * blk, blk)
        return [(s, h, kstart) for h in range(hb)]

    @pl.when(fast)
    def _():
        tiles([job for s in range(qb) for job in jobs_for(s, g0 + s - 2)], 3, True, True)

    @pl.when(jnp.logical_not(fast))
    def _():
        tiles([job for s in range(qb) for job in jobs_for(s, g0 + s)], 1, True, True)

    jbase = g0 - jnp.where(fast, 3, 1)
    nmax = jbase + qb - 1

    def cond(st):
        n, mx = st
        return jnp.logical_and(n <= nmax, mx > -F32_EXP_UNDERFLOW)

    def body(st):
        n, _ = st
        for s in range(qb):
            j = jbase + s - n

            @pl.when(j >= 0)
            def _():
                tiles(jobs_for(s, j), 1, False, False)

        return n + 1, jnp.max(carry_scr[...])

    lax.while_loop(cond, body, (jnp.int32(0), jnp.max(carry_scr[...])))
    for s in range(qb):
        for h in range(hb):
            o_ref[s * blk:(s + 1) * blk, h * blk:(h + 1) * blk] = acc_scr[s, h].astype(o_ref.dtype)


def _attn(qkv, batch, seq, n_heads, qb=2, hb=4):
    t = qkv.shape[0]
    blk = HEAD_DIM
    tq = qb * blk
    nq = seq // tq
    ng = n_heads // hb
    first_fast = -(-2 // qb)
    kern = functools.partial(_attn_kernel, qb=qb, hb=hb, first_fast=first_fast, scale=1.0 / math.sqrt(HEAD_DIM))
    once = pl.Buffered(1)
    m = _cumsum_matrix(blk)
    return pl.pallas_call(
        kern,
        grid=(batch, ng, nq),
        in_specs=[pl.BlockSpec((tq, hb * blk), lambda b, g, i: (b * nq + i, g)),
                  pl.BlockSpec((seq, hb * blk), lambda b, g, i: (b, ng + g), pipeline_mode=once),
                  pl.BlockSpec((seq, hb * blk), lambda b, g, i: (b, 2 * ng + g), pipeline_mode=once),
                  pl.BlockSpec(m.shape, lambda b, g, i: (0, 0), pipeline_mode=once)],
        out_specs=pl.BlockSpec((tq, hb * blk), lambda b, g, i: (b * nq + i, g)),
        out_shape=jax.ShapeDtypeStruct((t, n_heads * blk), BF16),
        scratch_shapes=[pltpu.VMEM((qb, hb, blk, blk), F32), pltpu.VMEM((qb, hb, blk, blk), F32)],
        compiler_params=_params(("arbitrary", "arbitrary", "arbitrary")),
        name="attn",
    )(qkv, qkv, qkv, m)


def _sigmoid(x):
    return 0.5 * jnp.tanh(0.5 * x) + 0.5


def _lru_phases(x_ref, gate_ref, cw_ref, cb_ref, wg_ref, bg_ref, lam_ref, out_ref, xs, us, a_s, b_s, h_st,
                *, tt, conv_w, blk, pitch, n_chunks, gate_rows):
    hist = SUBLANES
    chunk = tt // n_chunks
    step = lambda ref, t: ref.at[pl.ds(t, SUBLANES, stride=pitch), :]
    slab = lambda n, off=0: slice(n * pitch + off, n * pitch + off + tt)
    st = {}

    def conv(c):
        if c == 0:
            for n in range(SUBLANES):
                xs[slab(n, hist), :] = x_ref[:, n * blk:(n + 1) * blk]
            st["taps"] = [cw_ref[k] for k in range(conv_w)]
            st["win"] = [step(xs, hist - (conv_w - 1) + k)[...] for k in range(conv_w - 1)]
        win = st["win"]
        for t in range(c * chunk, (c + 1) * chunk):
            win = win + [step(xs, hist + t)[...]]
            u = cb_ref[...]
            for k in range(conv_w):
                u = u + st["taps"][k] * win[k]
            step(us, t)[...] = u
            win = win[1:]
        st["win"] = win
        if c == n_chunks - 1:
            for n in range(SUBLANES):
                xs[n * pitch:n * pitch + hist, :] = xs[n * pitch + tt:n * pitch + tt + hist, :]

    def gate(p):
        n, r = divmod(p, tt // gate_rows)
        rows = slice(n * pitch + r * gate_rows, n * pitch + (r + 1) * gate_rows)
        un = us[rows, :]
        g = _dot(un.astype(BF16), wg_ref[n]) + bg_ref[n]
        log_a = LRU_C * _sigmoid(g[:, :blk]) * jax.nn.log_sigmoid(lam_ref[:, n * blk:(n + 1) * blk])
        a = jnp.exp(log_a)
        a_s[rows, :] = a
        b_s[rows, :] = jnp.sqrt(-jnp.tanh(log_a) * (a * a + 1.0)) * (_sigmoid(g[:, blk:]) * un)

    def scan(c):
        h = h_st[...] if c == 0 else st["h"]
        for t in range(c * chunk, (c + 1) * chunk):
            h = step(a_s, t)[...] * h + step(b_s, t)[...]
            step(b_s, t)[...] = h
        st["h"] = h
        if c == n_chunks - 1:
            h_st[...] = h

    def out(n):
        cols = slice(n * blk, (n + 1) * blk)
        out_ref[:, cols] = (b_s[slab(n), :] * jax.nn.gelu(gate_ref[:, cols])).astype(out_ref.dtype)

    bind = lambda f, count: [functools.partial(f, c) for c in range(count)]
    return (bind(conv, n_chunks), bind(gate, SUBLANES * (tt // gate_rows)), bind(scan, n_chunks),
            bind(out, SUBLANES))


def _mix_kernel(lx_ref, lg_ref, cw_ref, cb_ref, wg_ref, bg_ref, lam_ref,
                att_ref, ga_ref, gb_ref, x_ref, mod_ref, wa_ref, wb_ref, wo_ref, g2_ref,
                x1_ref, h2_ref, xs, us, a_s, b_s, h_st, yb_next, yb_cur, merged_scr,
                *, n_tiles, tiles_per_batch, lru, cw_a, cw_o):
    i = pl.program_id(0)
    d = x_ref.shape[1]

    @pl.when(i == 0)
    def _():
        yb_next[...] = jnp.zeros_like(yb_next)

    @pl.when(jnp.minimum(i, n_tiles - 1) % tiles_per_batch == 0)
    def _():
        for n in range(SUBLANES):
            xs[n * lru["pitch"]:n * lru["pitch"] + SUBLANES, :] = jnp.zeros((SUBLANES, xs.shape[1]), F32)
        h_st[...] = jnp.zeros_like(h_st)

    yb_cur[...] = yb_next[...]
    conv, gate, scan, out = _lru_phases(lx_ref, lg_ref, cw_ref, cb_ref, wg_ref, bg_ref, lam_ref, yb_next,
                                        xs, us, a_s, b_s, h_st, **lru)

    def branches(c):
        cs = slice(c * cw_a, (c + 1) * cw_a)
        ya = _dot(att_ref[...], wa_ref[:, cs])
        yb = _dot(yb_cur[...], wb_ref[:, cs])
        merged_scr[:, cs] = (_sigmoid(ga_ref[:, cs]) * ya + _sigmoid(gb_ref[:, cs]) * yb).astype(BF16)

    sq = []

    def outproj(c):
        cs = slice(c * cw_o, (c + 1) * cw_o)
        x1 = x_ref[:, cs] + mod_ref[2:3, cs] * _dot(merged_scr[...], wo_ref[:, cs])
        x1_ref[:, cs] = x1
        sq.append(jnp.sum(x1 * x1, axis=-1, keepdims=True))

    def norm2():
        y = x1_ref[...] * lax.rsqrt(sum(sq) * (1.0 / d) + NORM_EPS) * g2_ref[...]
        h2_ref[...] = (y * (1.0 + mod_ref[4:5, :]) + mod_ref[3:4, :]).astype(BF16)

    first = [functools.partial(branches, c) for c in range(d // cw_a)]
    second = [functools.partial(outproj, c) for c in range(d // cw_o)]
    for f in conv + gate + scan + out + first + second:
        f()
    norm2()


def _mix(att, rest, x2, mod, conv_w, conv_b, wg, bg, lam, wa, wb, wo, g2, seq, tm=256):
    t, d = x2.shape
    k, w = conv_w.shape
    n_blocks, blk = wg.shape[0], wg.shape[1]
    assert n_blocks == SUBLANES and blk == HEAD_DIM, "one 128-lane channel slab per sublane"
    wa_k = wa.shape[0]
    n_tiles = t // tm
    tiles_per_batch = seq // tm
    pitch = tm + 2 * SUBLANES if (tm // SUBLANES) % 2 else tm + SUBLANES
    lru = dict(tt=tm, conv_w=k, blk=blk, pitch=pitch, n_chunks=1, gate_rows=tm)
    kern = functools.partial(_mix_kernel, n_tiles=n_tiles, tiles_per_batch=tiles_per_batch, lru=lru,
                             cw_a=d, cw_o=d)
    const = lambda shape: pl.BlockSpec(shape, lambda i: (0,) * len(shape), pipeline_mode=pl.Buffered(1))
    li = lambda i: jnp.minimum(i, n_tiles - 1)
    mi = lambda i: jnp.maximum(i - 1, 0)
    slab = lambda: pltpu.VMEM((SUBLANES * pitch, blk), F32)
    return pl.pallas_call(
        kern,
        grid=(n_tiles + 1,),
        in_specs=[pl.BlockSpec((tm, w), lambda i: (li(i), 0)),
                  pl.BlockSpec((tm, w), lambda i: (li(i), 1)),
                  const((k, SUBLANES, blk)), const((SUBLANES, blk)),
                  const((n_blocks, blk, 2 * blk)), const((n_blocks, 1, 2 * blk)), const((1, w)),
                  pl.BlockSpec((tm, wa_k), lambda i: (mi(i), 0)),
                  pl.BlockSpec((tm, d), lambda i: (mi(i), 1)),
                  pl.BlockSpec((tm, d), lambda i: (mi(i), 2)),
                  pl.BlockSpec((tm, d), lambda i: (mi(i), 0)),
                  pl.BlockSpec((None, N_MOD, d), lambda i: (mi(i) // tiles_per_batch, 0, 0)),
                  const((wa_k, d)), const((w, d)), const((d, d)), const((1, d))],
        out_specs=[pl.BlockSpec((tm, d), lambda i: (mi(i), 0)),
                   pl.BlockSpec((tm, d), lambda i: (mi(i), 0))],
        out_shape=[jax.ShapeDtypeStruct((t, d), F32), jax.ShapeDtypeStruct((t, d), BF16)],
        scratch_shapes=[slab(), slab(), slab(), slab(), pltpu.VMEM((SUBLANES, blk), F32),
                        pltpu.VMEM((tm, w), BF16), pltpu.VMEM((tm, w), BF16), pltpu.VMEM((tm, d), BF16)],
        compiler_params=_params(("arbitrary",)),
        name="mix",
    )(rest, rest, conv_w.reshape(k, SUBLANES, blk), conv_b.reshape(SUBLANES, blk), wg, bg, lam,
      att, rest, rest, x2, mod, wa, wb, wo, g2)


def _mlp_kernel(h2_ref, x1_ref, mod_ref, w1_ref, w2_ref, o_ref, acc_scr):
    f = pl.program_id(1)

    @pl.when(f == 0)
    def _():
        acc_scr[...] = jnp.zeros_like(acc_scr)

    a = jnp.maximum(_dot(h2_ref[...], w1_ref[...]), 0.0)
    acc_scr[...] += _dot((a * a).astype(BF16), w2_ref[...])

    @pl.when(f == pl.num_programs(1) - 1)
    def _():
        o_ref[...] = x1_ref[...] + mod_ref[5:6, :] * acc_scr[...]


def _mlp(h2, x1, mod, w1, w2, seq, tm=512, tf=1024):
    t, d = x1.shape
    dff = w1.shape[1]
    tiles_per_batch = seq // tm
    return pl.pallas_call(
        _mlp_kernel,
        grid=(t // tm, dff // tf),
        in_specs=[pl.BlockSpec((tm, d), lambda i, f: (i, 0)),
                  pl.BlockSpec((tm, d), lambda i, f: (i, 0)),
                  pl.BlockSpec((None, N_MOD, d), lambda i, f: (i // tiles_per_batch, 0, 0)),
                  pl.BlockSpec((d, tf), lambda i, f: (0, f)),
                  pl.BlockSpec((tf, d), lambda i, f: (f, 0))],
        out_specs=pl.BlockSpec((tm, d), lambda i, f: (i, 0)),
        out_shape=jax.ShapeDtypeStruct((t, d), F32),
        scratch_shapes=[pltpu.VMEM((tm, d), F32)],
        compiler_params=_params(("arbitrary", "arbitrary")),
        name="mlp",
    )(h2, x1, mod, w1, w2)


def kernel(x, c, w_ada, b_ada, norm1_g, w_in, b_in, q_norm_g, k_norm_g, conv_w, conv_b, w_rg, b_rg, w_ig, b_ig,
           lru_lambda, w_branch_a, w_branch_b, w_out, norm2_g, w_mlp_in, w_mlp_out):
    batch, seq, d = x.shape
    depth = w_ada.shape[0]
    n_blocks, blk = w_rg.shape[1], w_rg.shape[2]
    n_heads = w_branch_a.shape[1] // HEAD_DIM
    row = lambda v: v.reshape(1, -1)

    x2 = x.reshape(batch * seq, d)
    c_pad = jnp.zeros((SUBLANES, d), F32).at[:batch].set(c)
    for l in range(depth):
        mod = _ada(c_pad, w_ada[l], row(b_ada[l]))[:batch].reshape(batch, N_MOD, d)
        qkv, rest = _inproj(x2, mod, row(norm1_g[l]), w_in[l].astype(BF16), row(b_in[l]),
                            row(q_norm_g[l]), row(k_norm_g[l]), seq)
        att = _attn(qkv, batch, seq, n_heads)
        wg = jnp.concatenate([w_rg[l], w_ig[l]], axis=-1).astype(BF16)
        bg = jnp.concatenate([b_rg[l].reshape(n_blocks, 1, blk), b_ig[l].reshape(n_blocks, 1, blk)], axis=-1)
        x1, h2 = _mix(att, rest, x2, mod, conv_w[l], row(conv_b[l]), wg, bg, row(lru_lambda[l]),
                      w_branch_a[l].astype(BF16), w_branch_b[l].astype(BF16), w_out[l].astype(BF16),
                      row(norm2_g[l]), seq)
        x2 = _mlp(h2, x1, mod, w_mlp_in[l].astype(BF16), w_mlp_out[l].astype(BF16), seq)
    return x2.reshape(batch, seq, d)
```

```python
import functools
import math

import jax
import jax.numpy as jnp
from jax import lax
from jax.experimental import pallas as pl
from jax.experimental.pallas import tpu as pltpu

F32 = jnp.float32
BF16 = jnp.bfloat16

NORM_EPS = 1e-6
LRU_C = 8.0
HEAD_DIM = 128
N_MOD = 6
SUBLANES = 8
F32_EXP_UNDERFLOW = 104.0
VMEM_LIMIT = 56 * 1024 * 1024


def _params(sem, vmem=VMEM_LIMIT):
    return pltpu.CompilerParams(dimension_semantics=sem, vmem_limit_bytes=vmem)


def _dot(a, b):
    return jnp.dot(a, b, preferred_element_type=F32)


def _ada_kernel(c_ref, w_ref, b_ref, o_ref):
    ca = jax.nn.silu(c_ref[...]).astype(BF16)
    o_ref[...] = _dot(ca, w_ref[...].astype(BF16)) + b_ref[...]


def _ada(c_pad, w_ada, b_ada, tn=1024):
    m, d = c_pad.shape
    n = w_ada.shape[1]
    return pl.pallas_call(
        _ada_kernel,
        grid=(n // tn,),
        in_specs=[pl.BlockSpec((m, d), lambda j: (0, 0)),
                  pl.BlockSpec((d, tn), lambda j: (0, j)),
                  pl.BlockSpec((1, tn), lambda j: (0, j))],
        out_specs=pl.BlockSpec((m, tn), lambda j: (0, j)),
        out_shape=jax.ShapeDtypeStruct((m, n), F32),
        compiler_params=_params(("arbitrary",)),
        name="ada",
    )(c_pad, w_ada, b_ada)


def _inproj_kernel(x_ref, mod_ref, g1_ref, w_ref, b_ref, qg_ref, kg_ref, obf_ref, of_ref, h_scr, *, n_heads, nf):
    j = pl.program_id(1)

    @pl.when(j == 0)
    def _():
        x = x_ref[...]
        ms = jnp.mean(x * x, axis=-1, keepdims=True)
        y = x * lax.rsqrt(ms + NORM_EPS) * g1_ref[...]
        h_scr[...] = (y * (1.0 + mod_ref[1:2, :]) + mod_ref[0:1, :]).astype(BF16)

    of_ref[...] = _dot(h_scr[...], w_ref[...]) + b_ref[...]

    @pl.when(jnp.logical_and(j >= nf, j < nf + 2))
    def _():
        g = jnp.where(j == nf, qg_ref[...], kg_ref[...])
        for hh in range(n_heads):
            sl = slice(hh * HEAD_DIM, (hh + 1) * HEAD_DIM)
            a = of_ref[:, sl]
            ms = jnp.mean(a * a, axis=-1, keepdims=True)
            obf_ref[:, sl] = (a * lax.rsqrt(ms + NORM_EPS) * g).astype(BF16)

    @pl.when(j == nf + 2)
    def _():
        obf_ref[...] = of_ref[...].astype(BF16)


def _inproj(x2, mod, g1, w_in, b_in, qg, kg, seq, tm=1024, tn=1024):
    t, d = x2.shape
    n = w_in.shape[1]
    nj = n // tn
    nf = nj - 3
    tiles_per_batch = seq // tm
    kern = functools.partial(_inproj_kernel, n_heads=tn // HEAD_DIM, nf=nf)
    wcol = lambda j: (j + 3) % nj
    return pl.pallas_call(
        kern,
        grid=(t // tm, nj),
        in_specs=[pl.BlockSpec((tm, d), lambda i, j: (i, 0)),
                  pl.BlockSpec((None, N_MOD, d), lambda i, j: (i // tiles_per_batch, 0, 0)),
                  pl.BlockSpec((1, d), lambda i, j: (0, 0)),
                  pl.BlockSpec((d, tn), lambda i, j: (0, wcol(j))),
                  pl.BlockSpec((1, tn), lambda i, j: (0, wcol(j))),
                  pl.BlockSpec((1, HEAD_DIM), lambda i, j: (0, 0)),
                  pl.BlockSpec((1, HEAD_DIM), lambda i, j: (0, 0))],
        out_specs=[pl.BlockSpec((tm, tn), lambda i, j: (i, jnp.maximum(j - nf, 0))),
                   pl.BlockSpec((tm, tn), lambda i, j: (i, jnp.minimum(j, nf)))],
        out_shape=[jax.ShapeDtypeStruct((t, 3 * tn), BF16),
                   jax.ShapeDtypeStruct((t, (nf + 1) * tn), F32)],
        scratch_shapes=[pltpu.VMEM((tm, d), BF16)],
        compiler_params=_params(("arbitrary", "arbitrary")),
        name="inproj",
    )(x2, mod, g1, w_in, b_in, qg, kg)


def _cumsum_matrix(width):
    j = jnp.arange(width)
    m = jnp.concatenate([(j[:, None] >= j[None, :]).astype(BF16), jnp.ones((width, HEAD_DIM), BF16)], axis=1)
    return jnp.concatenate([m, m], axis=0)


def _attn_kernel(q_ref, k_ref, v_ref, m_ref, *refs, n_cast, qb, hb, first_fast, scale):
    cast_in, o_ref, cast_out = refs[:n_cast], refs[n_cast], refs[n_cast + 1:2 * n_cast + 1]
    carry_scr, acc_scr = refs[2 * n_cast + 1:]
    for src, dst in zip(cast_in, cast_out):
        dst[...] = src[...].astype(dst.dtype)

    i = pl.program_id(2)
    blk = HEAD_DIM
    row = lax.broadcasted_iota(jnp.int32, (blk, blk), 0)
    col = lax.broadcasted_iota(jnp.int32, (blk, blk), 1)
    causal = col < row

    def tiles(jobs, nb, diag, first):
        sl = lambda n: slice(n * blk, (n + 1) * blk)
        zs = [lax.dot_general(q_ref[sl(s), sl(h)], k_ref[pl.ds(kstart, nb * blk), sl(h)],
                              (((1,), (1,)), ((), ())), preferred_element_type=F32) * scale
              for s, h, kstart in jobs]
        lhs = []
        for z in zs:
            lom = -(jnp.maximum(z, 0.0) + jnp.log(1.0 + jnp.exp(-jnp.abs(z))))
            row_lhs = []
            for c in range(nb):
                lom_c = jnp.where(causal, lom[:, sl(c)], 0.0) if (diag and c == nb - 1) else lom[:, sl(c)]
                hi = lom_c.astype(BF16)
                lo = (lom_c - hi.astype(F32)).astype(BF16)
                row_lhs.append(jnp.concatenate([hi, lo], axis=1))
            lhs.append(row_lhs)
        boths = [[_dot(a, m_ref[...]) for a in row_lhs] for row_lhs in lhs]
        ws = []
        for (s, h, _), z, both in zip(jobs, zs, boths):
            carry = None if first else carry_scr[s, h]
            w = [None] * nb
            for c in reversed(range(nb)):
                logw = z[:, sl(c)] + both[c][:, :blk]
                if carry is not None:
                    logw = logw + carry
                e = jnp.exp(logw)
                w[c] = (jnp.where(causal, e, 0.0) if (diag and c == nb - 1) else e).astype(BF16)
                carry = both[c][:, blk:] if carry is None else carry + both[c][:, blk:]
            carry_scr[s, h] = carry
            ws.append(w[0] if nb == 1 else jnp.concatenate(w, axis=1))
        for (s, h, kstart), w in zip(jobs, ws):
            pv = _dot(w, v_ref[pl.ds(kstart, nb * blk), sl(h)])
            acc_scr[s, h] = pv if first else acc_scr[s, h] + pv

    g0 = i * qb
    fast = i >= first_fast

    def jobs_for(s, block):
        kstart = pl.multiple_of(block * blk, blk)
        return [(s, h, kstart) for h in range(hb)]

    @pl.when(fast)
    def _():
        tiles([job for s in range(qb) for job in jobs_for(s, g0 + s - 2)], 3, True, True)

    @pl.when(jnp.logical_not(fast))
    def _():
        tiles([job for s in range(qb) for job in jobs_for(s, g0 + s)], 1, True, True)

    jbase = g0 - jnp.where(fast, 3, 1)
    nmax = jbase + qb - 1

    def cond(st):
        n, mx = st
        return jnp.logical_and(n <= nmax, mx > -F32_EXP_UNDERFLOW)

    def body(st):
        n, _ = st
        for s in range(qb):
            j = jbase + s - n

            @pl.when(j >= 0)
            def _():
                tiles(jobs_for(s, j), 1, False, False)

        return n + 1, jnp.max(carry_scr[...])

    lax.while_loop(cond, body, (jnp.int32(0), jnp.max(carry_scr[...])))
    for s in range(qb):
        for h in range(hb):
            o_ref[s * blk:(s + 1) * blk, h * blk:(h + 1) * blk] = acc_scr[s, h].astype(o_ref.dtype)


def _attn(qkv, batch, seq, n_heads, to_bf16=(), qb=4, hb=4):
    t = qkv.shape[0]
    blk = HEAD_DIM
    tq = qb * blk
    nq = seq // tq
    ng = n_heads // hb
    n_steps = batch * ng * nq
    first_fast = -(-2 // qb)
    kern = functools.partial(_attn_kernel, n_cast=len(to_bf16), qb=qb, hb=hb, first_fast=first_fast,
                             scale=1.0 / math.sqrt(HEAD_DIM))
    once = pl.Buffered(1)
    m = _cumsum_matrix(blk)
    bf16_rows = 2 * SUBLANES
    for w in to_bf16:
        assert w.shape[0] % (n_steps * bf16_rows) == 0, "row slices must be whole bf16 tiles"
    slices = [pl.BlockSpec((w.shape[0] // n_steps, w.shape[1]), lambda b, g, i: ((b * ng + g) * nq + i, 0))
              for w in to_bf16]
    att, *copies = pl.pallas_call(
        kern,
        grid=(batch, ng, nq),
        in_specs=[pl.BlockSpec((tq, hb * blk), lambda b, g, i: (b * nq + i, g)),
                  pl.BlockSpec((seq, hb * blk), lambda b, g, i: (b, ng + g), pipeline_mode=once),
                  pl.BlockSpec((seq, hb * blk), lambda b, g, i: (b, 2 * ng + g), pipeline_mode=once),
                  pl.BlockSpec(m.shape, lambda b, g, i: (0, 0), pipeline_mode=once)] + slices,
        out_specs=[pl.BlockSpec((tq, hb * blk), lambda b, g, i: (b * nq + i, g))] + slices,
        out_shape=[jax.ShapeDtypeStruct((t, n_heads * blk), BF16)]
        + [jax.ShapeDtypeStruct(w.shape, BF16) for w in to_bf16],
        scratch_shapes=[pltpu.VMEM((qb, hb, blk, blk), F32), pltpu.VMEM((qb, hb, blk, blk), F32)],
        compiler_params=_params(("arbitrary", "arbitrary", "arbitrary")),
        name="attn",
    )(qkv, qkv, qkv, m, *to_bf16)
    return att, copies


def _sigmoid(x):
    return 0.5 * jnp.tanh(0.5 * x) + 0.5


def _lru_phases(x_ref, gate_ref, cw_ref, cb_ref, wg_ref, bg_ref, lam_ref, out_ref, xs, us, a_s, b_s, h_st,
                *, tt, conv_w, blk, pitch, n_chunks, gate_rows):
    hist = SUBLANES
    chunk = tt // n_chunks
    step = lambda ref, t: ref.at[pl.ds(t, SUBLANES, stride=pitch), :]
    slab = lambda n, off=0: slice(n * pitch + off, n * pitch + off + tt)
    st = {}

    def conv(c):
        if c == 0:
            for n in range(SUBLANES):
                xs[slab(n, hist), :] = x_ref[:, n * blk:(n + 1) * blk]
            st["taps"] = [cw_ref[k] for k in range(conv_w)]
            st["win"] = [step(xs, hist - (conv_w - 1) + k)[...] for k in range(conv_w - 1)]
        win = st["win"]
        for t in range(c * chunk, (c + 1) * chunk):
            win = win + [step(xs, hist + t)[...]]
            u = cb_ref[...]
            for k in range(conv_w):
                u = u + st["taps"][k] * win[k]
            step(us, t)[...] = u
            win = win[1:]
        st["win"] = win
        if c == n_chunks - 1:
            for n in range(SUBLANES):
                xs[n * pitch:n * pitch + hist, :] = xs[n * pitch + tt:n * pitch + tt + hist, :]

    def gate(p):
        n, r = divmod(p, tt // gate_rows)
        rows = slice(n * pitch + r * gate_rows, n * pitch + (r + 1) * gate_rows)
        un = us[rows, :]
        g = _dot(un.astype(BF16), wg_ref[n]) + bg_ref[n]
        log_a = LRU_C * _sigmoid(g[:, :blk]) * jax.nn.log_sigmoid(lam_ref[:, n * blk:(n + 1) * blk])
        a = jnp.exp(log_a)
        a_s[rows, :] = a
        b_s[rows, :] = jnp.sqrt(-jnp.tanh(log_a) * (a * a + 1.0)) * (_sigmoid(g[:, blk:]) * un)

    def scan(c):
        h = h_st[...] if c == 0 else st["h"]
        for t in range(c * chunk, (c + 1) * chunk):
            h = step(a_s, t)[...] * h + step(b_s, t)[...]
            step(b_s, t)[...] = h
        st["h"] = h
        if c == n_chunks - 1:
            h_st[...] = h

    def out(n):
        cols = slice(n * blk, (n + 1) * blk)
        out_ref[:, cols] = (b_s[slab(n), :] * jax.nn.gelu(gate_ref[:, cols])).astype(out_ref.dtype)

    bind = lambda f, count: [functools.partial(f, c) for c in range(count)]
    return (bind(conv, n_chunks), bind(gate, SUBLANES * (tt // gate_rows)), bind(scan, n_chunks),
            bind(out, SUBLANES))


def _mix_kernel(lx_ref, lg_ref, cw_ref, cb_ref, wg_ref, bg_ref, lam_ref,
                att_ref, ga_ref, gb_ref, x_ref, mod_ref, wa_ref, wb_ref, wo_ref, g2_ref,
                x1_ref, h2_ref, xs, us, a_s, b_s, h_st, yb_next, yb_cur, merged_scr,
                *, n_tiles, tiles_per_batch, lru, cw_a, cw_o):
    i = pl.program_id(0)
    d = x_ref.shape[1]

    @pl.when(i == 0)
    def _():
        yb_next[...] = jnp.zeros_like(yb_next)

    @pl.when(jnp.minimum(i, n_tiles - 1) % tiles_per_batch == 0)
    def _():
        for n in range(SUBLANES):
            xs[n * lru["pitch"]:n * lru["pitch"] + SUBLANES, :] = jnp.zeros((SUBLANES, xs.shape[1]), F32)
        h_st[...] = jnp.zeros_like(h_st)

    yb_cur[...] = yb_next[...]
    conv, gate, scan, out = _lru_phases(lx_ref, lg_ref, cw_ref, cb_ref, wg_ref, bg_ref, lam_ref, yb_next,
                                        xs, us, a_s, b_s, h_st, **lru)

    def branches(c):
        cs = slice(c * cw_a, (c + 1) * cw_a)
        ya = _dot(att_ref[...], wa_ref[:, cs])
        yb = _dot(yb_cur[...], wb_ref[:, cs])
        merged_scr[:, cs] = (_sigmoid(ga_ref[:, cs]) * ya + _sigmoid(gb_ref[:, cs]) * yb).astype(BF16)

    sq = []

    def outproj(c):
        cs = slice(c * cw_o, (c + 1) * cw_o)
        x1 = x_ref[:, cs] + mod_ref[2:3, cs] * _dot(merged_scr[...], wo_ref[:, cs])
        x1_ref[:, cs] = x1
        sq.append(jnp.sum(x1 * x1, axis=-1, keepdims=True))

    def norm2():
        y = x1_ref[...] * lax.rsqrt(sum(sq) * (1.0 / d) + NORM_EPS) * g2_ref[...]
        h2_ref[...] = (y * (1.0 + mod_ref[4:5, :]) + mod_ref[3:4, :]).astype(BF16)

    first = [functools.partial(branches, c) for c in range(d // cw_a)]
    second = [functools.partial(outproj, c) for c in range(d // cw_o)]
    for f in conv + gate + scan + out + first + second:
        f()
    norm2()


def _mix(att, rest, x2, mod, conv_w, conv_b, wg, bg, lam, wa, wb, wo, g2, seq, tm=256):
    t, d = x2.shape
    k, w = conv_w.shape
    n_blocks, blk = wg.shape[0], wg.shape[1]
    assert n_blocks == SUBLANES and blk == HEAD_DIM, "one 128-lane channel slab per sublane"
    wa_k = wa.shape[0]
    n_tiles = t // tm
    tiles_per_batch = seq // tm
    pitch = tm + 2 * SUBLANES if (tm // SUBLANES) % 2 else tm + SUBLANES
    lru = dict(tt=tm, conv_w=k, blk=blk, pitch=pitch, n_chunks=1, gate_rows=tm)
    kern = functools.partial(_mix_kernel, n_tiles=n_tiles, tiles_per_batch=tiles_per_batch, lru=lru,
                             cw_a=d, cw_o=d)
    const = lambda shape: pl.BlockSpec(shape, lambda i: (0,) * len(shape), pipeline_mode=pl.Buffered(1))
    li = lambda i: jnp.minimum(i, n_tiles - 1)
    mi = lambda i: jnp.maximum(i - 1, 0)
    slab = lambda: pltpu.VMEM((SUBLANES * pitch, blk), F32)
    return pl.pallas_call(
        kern,
        grid=(n_tiles + 1,),
        in_specs=[pl.BlockSpec((tm, w), lambda i: (li(i), 0)),
                  pl.BlockSpec((tm, w), lambda i: (li(i), 1)),
                  const((k, SUBLANES, blk)), const((SUBLANES, blk)),
                  const((n_blocks, blk, 2 * blk)), const((n_blocks, 1, 2 * blk)), const((1, w)),
                  pl.BlockSpec((tm, wa_k), lambda i: (mi(i), 0)),
                  pl.BlockSpec((tm, d), lambda i: (mi(i), 1)),
                  pl.BlockSpec((tm, d), lambda i: (mi(i), 2)),
                  pl.BlockSpec((tm, d), lambda i: (mi(i), 0)),
                  pl.BlockSpec((None, N_MOD, d), lambda i: (mi(i) // tiles_per_batch, 0, 0)),
                  const((wa_k, d)), const((w, d)), const((d, d)), const((1, d))],
        out_specs=[pl.BlockSpec((tm, d), lambda i: (mi(i), 0)),
                   pl.BlockSpec((tm, d), lambda i: (mi(i), 0))],
        out_shape=[jax.ShapeDtypeStruct((t, d), F32), jax.ShapeDtypeStruct((t, d), BF16)],
        scratch_shapes=[slab(), slab(), slab(), slab(), pltpu.VMEM((SUBLANES, blk), F32),
                        pltpu.VMEM((tm, w), BF16), pltpu.VMEM((tm, w), BF16), pltpu.VMEM((tm, d), BF16)],
        compiler_params=_params(("arbitrary",)),
        name="mix",
    )(rest, rest, conv_w.reshape(k, SUBLANES, blk), conv_b.reshape(SUBLANES, blk), wg, bg, lam,
      att, rest, rest, x2, mod, wa, wb, wo, g2)


def _mlp_kernel(h2_ref, x1_ref, mod_ref, w1_ref, w2_ref, o_ref, acc_scr):
    f = pl.program_id(1)

    @pl.when(f == 0)
    def _():
        acc_scr[...] = jnp.zeros_like(acc_scr)

    a = jnp.maximum(_dot(h2_ref[...], w1_ref[...]), 0.0)
    acc_scr[...] += _dot((a * a).astype(BF16), w2_ref[...])

    @pl.when(f == pl.num_programs(1) - 1)
    def _():
        o_ref[...] = x1_ref[...] + mod_ref[5:6, :] * acc_scr[...]


def _mlp(h2, x1, mod, w1, w2, seq, tm=512, tf=1024):
    t, d = x1.shape
    dff = w1.shape[1]
    tiles_per_batch = seq // tm
    return pl.pallas_call(
        _mlp_kernel,
        grid=(t // tm, dff // tf),
        in_specs=[pl.BlockSpec((tm, d), lambda i, f: (i, 0)),
                  pl.BlockSpec((tm, d), lambda i, f: (i, 0)),
                  pl.BlockSpec((None, N_MOD, d), lambda i, f: (i // tiles_per_batch, 0, 0)),
                  pl.BlockSpec((d, tf), lambda i, f: (0, f)),
                  pl.BlockSpec((tf, d), lambda i, f: (f, 0))],
        out_specs=pl.BlockSpec((tm, d), lambda i, f: (i, 0)),
        out_shape=jax.ShapeDtypeStruct((t, d), F32),
        scratch_shapes=[pltpu.VMEM((tm, d), F32)],
        compiler_params=_params(("arbitrary", "arbitrary")),
        name="mlp",
    )(h2, x1, mod, w1, w2)


def kernel(x, c, w_ada, b_ada, norm1_g, w_in, b_in, q_norm_g, k_norm_g, conv_w, conv_b, w_rg, b_rg, w_ig, b_ig,
           lru_lambda, w_branch_a, w_branch_b, w_out, norm2_g, w_mlp_in, w_mlp_out):
    batch, seq, d = x.shape
    depth = w_ada.shape[0]
    n_blocks, blk = w_rg.shape[1], w_rg.shape[2]
    n_heads = w_branch_a.shape[1] // HEAD_DIM
    row = lambda v: v.reshape(1, -1)

    x2 = x.reshape(batch * seq, d)
    c_pad = jnp.zeros((SUBLANES, d), F32).at[:batch].set(c)
    for l in range(depth):
        mod = _ada(c_pad, w_ada[l], row(b_ada[l]))[:batch].reshape(batch, N_MOD, d)
        qkv, rest = _inproj(x2, mod, row(norm1_g[l]), w_in[l].astype(BF16), row(b_in[l]),
                            row(q_norm_g[l]), row(k_norm_g[l]), seq)
        att, (wo, w1, w2) = _attn(qkv, batch, seq, n_heads, (w_out[l], w_mlp_in[l], w_mlp_out[l]))
        wg = jnp.concatenate([w_rg[l], w_ig[l]], axis=-1).astype(BF16)
        bg = jnp.concatenate([b_rg[l].reshape(n_blocks, 1, blk), b_ig[l].reshape(n_blocks, 1, blk)], axis=-1)
        x1, h2 = _mix(att, rest, x2, mod, conv_w[l], row(conv_b[l]), wg, bg, row(lru_lambda[l]),
                      w_branch_a[l].astype(BF16), w_branch_b[l].astype(BF16), wo, row(norm2_g[l]), seq)
        x2 = _mlp(h2, x1, mod, w1, w2, seq)
    return x2.reshape(batch, seq, d)
```

```python
import functools
import math

import jax
import jax.numpy as jnp
from jax import lax
from jax.experimental import pallas as pl
from jax.experimental.pallas import tpu as pltpu

F32 = jnp.float32
BF16 = jnp.bfloat16

NORM_EPS = 1e-6
LRU_C = 8.0
GELU_K = math.sqrt(2.0 / math.pi)
HEAD_DIM = 128
N_MOD = 6
SUBLANES = 8
F32_EXP_UNDERFLOW = 104.0
VMEM_LIMIT = 56 * 1024 * 1024


def _params(sem, vmem=VMEM_LIMIT):
    return pltpu.CompilerParams(dimension_semantics=sem, vmem_limit_bytes=vmem)


def _dot(a, b):
    return jnp.dot(a, b, preferred_element_type=F32)


def _ada_kernel(c_ref, w_ref, b_ref, o_ref):
    ca = jax.nn.silu(c_ref[...]).astype(BF16)
    o_ref[...] = _dot(ca, w_ref[...].astype(BF16)) + b_ref[...]


def _ada(c_pad, w_ada, b_ada, tn=1024):
    m, d = c_pad.shape
    n = w_ada.shape[1]
    return pl.pallas_call(
        _ada_kernel,
        grid=(n // tn,),
        in_specs=[pl.BlockSpec((m, d), lambda j: (0, 0)),
                  pl.BlockSpec((d, tn), lambda j: (0, j)),
                  pl.BlockSpec((1, tn), lambda j: (0, j))],
        out_specs=pl.BlockSpec((m, tn), lambda j: (0, j)),
        out_shape=jax.ShapeDtypeStruct((m, n), F32),
        compiler_params=_params(("arbitrary",)),
        name="ada",
    )(c_pad, w_ada, b_ada)


def _inproj_kernel(x_ref, mod_ref, g1_ref, w_ref, b_ref, qg_ref, kg_ref, obf_ref, of_ref, h_scr, *, n_heads, nf):
    j = pl.program_id(1)

    @pl.when(j == 0)
    def _():
        x = x_ref[...]
        ms = jnp.mean(x * x, axis=-1, keepdims=True)
        gain = g1_ref[...] * (1.0 + mod_ref[1:2, :])
        h_scr[...] = (x * lax.rsqrt(ms + NORM_EPS) * gain + mod_ref[0:1, :]).astype(BF16)

    of_ref[...] = _dot(h_scr[...], w_ref[...]) + b_ref[...]

    @pl.when(jnp.logical_and(j >= nf, j < nf + 2))
    def _():
        g = jnp.where(j == nf, qg_ref[...], kg_ref[...])
        for hh in range(n_heads):
            sl = slice(hh * HEAD_DIM, (hh + 1) * HEAD_DIM)
            a = of_ref[:, sl]
            ms = jnp.mean(a * a, axis=-1, keepdims=True)
            obf_ref[:, sl] = (a * lax.rsqrt(ms + NORM_EPS) * g).astype(BF16)

    @pl.when(j == nf + 2)
    def _():
        obf_ref[...] = of_ref[...].astype(BF16)


def _inproj(x2, mod, g1, w_in, b_in, qg, kg, seq, tm=1024, tn=1024):
    t, d = x2.shape
    n = w_in.shape[1]
    nj = n // tn
    nf = nj - 3
    tiles_per_batch = seq // tm
    kern = functools.partial(_inproj_kernel, n_heads=tn // HEAD_DIM, nf=nf)
    wcol = lambda j: (j + 3) % nj
    return pl.pallas_call(
        kern,
        grid=(t // tm, nj),
        in_specs=[pl.BlockSpec((tm, d), lambda i, j: (i, 0)),
                  pl.BlockSpec((None, N_MOD, d), lambda i, j: (i // tiles_per_batch, 0, 0)),
                  pl.BlockSpec((1, d), lambda i, j: (0, 0)),
                  pl.BlockSpec((d, tn), lambda i, j: (0, wcol(j))),
                  pl.BlockSpec((1, tn), lambda i, j: (0, wcol(j))),
                  pl.BlockSpec((1, HEAD_DIM), lambda i, j: (0, 0)),
                  pl.BlockSpec((1, HEAD_DIM), lambda i, j: (0, 0))],
        out_specs=[pl.BlockSpec((tm, tn), lambda i, j: (i, jnp.maximum(j - nf, 0))),
                   pl.BlockSpec((tm, tn), lambda i, j: (i, jnp.minimum(j, nf)))],
        out_shape=[jax.ShapeDtypeStruct((t, 3 * tn), BF16),
                   jax.ShapeDtypeStruct((t, (nf + 1) * tn), F32)],
        scratch_shapes=[pltpu.VMEM((tm, d), BF16)],
        compiler_params=_params(("arbitrary", "arbitrary")),
        name="inproj",
    )(x2, mod, g1, w_in, b_in, qg, kg)


def _cumsum_matrix(width):
    j = jnp.arange(width)
    m = jnp.concatenate([(j[:, None] >= j[None, :]).astype(BF16), jnp.ones((width, HEAD_DIM), BF16)], axis=1)
    return jnp.concatenate([m, m], axis=0)


def _attn_kernel(q_ref, k_ref, v_ref, m_ref, *refs, n_cast, qb, hb, first_fast, scale):
    cast_in, o_ref, cast_out = refs[:n_cast], refs[n_cast], refs[n_cast + 1:2 * n_cast + 1]
    carry_scr, acc_scr = refs[2 * n_cast + 1:]
    for src, dst in zip(cast_in, cast_out):
        dst[...] = src[...].astype(dst.dtype)

    i = pl.program_id(2)
    blk = HEAD_DIM
    row = lax.broadcasted_iota(jnp.int32, (blk, blk), 0)
    col = lax.broadcasted_iota(jnp.int32, (blk, blk), 1)
    causal = col < row

    def tiles(jobs, nb, diag, first):
        sl = lambda n: slice(n * blk, (n + 1) * blk)
        zs = [lax.dot_general(q_ref[sl(s), sl(h)], k_ref[pl.ds(kstart, nb * blk), sl(h)],
                              (((1,), (1,)), ((), ())), preferred_element_type=F32) * scale
              for s, h, kstart in jobs]
        lhs = []
        for z in zs:
            lom = -(jnp.maximum(z, 0.0) + jnp.log(1.0 + jnp.exp(-jnp.abs(z))))
            row_lhs = []
            for c in range(nb):
                lom_c = jnp.where(causal, lom[:, sl(c)], 0.0) if (diag and c == nb - 1) else lom[:, sl(c)]
                hi = lom_c.astype(BF16)
                lo = (lom_c - hi.astype(F32)).astype(BF16)
                row_lhs.append(jnp.concatenate([hi, lo], axis=1))
            lhs.append(row_lhs)
        boths = [[_dot(a, m_ref[...]) for a in row_lhs] for row_lhs in lhs]
        ws = []
        for (s, h, _), z, both in zip(jobs, zs, boths):
            carry = None if first else carry_scr[s, h]
            w = [None] * nb
            for c in reversed(range(nb)):
                logw = z[:, sl(c)] + both[c][:, :blk]
                if carry is not None:
                    logw = logw + carry
                e = jnp.exp(logw)
                w[c] = (jnp.where(causal, e, 0.0) if (diag and c == nb - 1) else e).astype(BF16)
                carry = both[c][:, blk:] if carry is None else carry + both[c][:, blk:]
            carry_scr[s, h] = carry
            ws.append(w[0] if nb == 1 else jnp.concatenate(w, axis=1))
        for (s, h, kstart), w in zip(jobs, ws):
            pv = _dot(w, v_ref[pl.ds(kstart, nb * blk), sl(h)])
            acc_scr[s, h] = pv if first else acc_scr[s, h] + pv

    g0 = i * qb
    fast = i >= first_fast

    def jobs_for(s, block):
        kstart = pl.multiple_of(block * blk, blk)
        return [(s, h, kstart) for h in range(hb)]

    @pl.when(fast)
    def _():
        tiles([job for s in range(qb) for job in jobs_for(s, g0 + s - 2)], 3, True, True)

    @pl.when(jnp.logical_not(fast))
    def _():
        tiles([job for s in range(qb) for job in jobs_for(s, g0 + s)], 1, True, True)

    jbase = g0 - jnp.where(fast, 3, 1)
    nmax = jbase + qb - 1

    def cond(st):
        n, mx = st
        return jnp.logical_and(n <= nmax, mx > -F32_EXP_UNDERFLOW)

    def body(st):
        n, _ = st
        for s in range(qb):
            j = jbase + s - n

            @pl.when(j >= 0)
            def _():
                tiles(jobs_for(s, j), 1, False, False)

        return n + 1, jnp.max(carry_scr[...])

    lax.while_loop(cond, body, (jnp.int32(0), jnp.max(carry_scr[...])))
    for s in range(qb):
        for h in range(hb):
            o_ref[s * blk:(s + 1) * blk, h * blk:(h + 1) * blk] = acc_scr[s, h].astype(o_ref.dtype)


def _attn(qkv, batch, seq, n_heads, to_bf16=(), qb=4, hb=4):
    t = qkv.shape[0]
    blk = HEAD_DIM
    tq = qb * blk
    nq = seq // tq
    ng = n_heads // hb
    n_steps = batch * ng * nq
    first_fast = -(-2 // qb)
    kern = functools.partial(_attn_kernel, n_cast=len(to_bf16), qb=qb, hb=hb, first_fast=first_fast,
                             scale=1.0 / math.sqrt(HEAD_DIM))
    once = pl.Buffered(1)
    m = _cumsum_matrix(blk)
    bf16_rows = 2 * SUBLANES
    for w in to_bf16:
        assert w.shape[0] % (n_steps * bf16_rows) == 0, "row slices must be whole bf16 tiles"
    slices = [pl.BlockSpec((w.shape[0] // n_steps, w.shape[1]), lambda b, g, i: ((b * ng + g) * nq + i, 0))
              for w in to_bf16]
    att, *copies = pl.pallas_call(
        kern,
        grid=(batch, ng, nq),
        in_specs=[pl.BlockSpec((tq, hb * blk), lambda b, g, i: (b * nq + i, g)),
                  pl.BlockSpec((seq, hb * blk), lambda b, g, i: (b, ng + g), pipeline_mode=once),
                  pl.BlockSpec((seq, hb * blk), lambda b, g, i: (b, 2 * ng + g), pipeline_mode=once),
                  pl.BlockSpec(m.shape, lambda b, g, i: (0, 0), pipeline_mode=once)] + slices,
        out_specs=[pl.BlockSpec((tq, hb * blk), lambda b, g, i: (b * nq + i, g))] + slices,
        out_shape=[jax.ShapeDtypeStruct((t, n_heads * blk), BF16)]
        + [jax.ShapeDtypeStruct(w.shape, BF16) for w in to_bf16],
        scratch_shapes=[pltpu.VMEM((qb, hb, blk, blk), F32), pltpu.VMEM((qb, hb, blk, blk), F32)],
        compiler_params=_params(("arbitrary", "arbitrary", "arbitrary")),
        name="attn",
    )(qkv, qkv, qkv, m, *to_bf16)
    return att, copies


def _sigmoid(x):
    return 0.5 * jnp.tanh(0.5 * x) + 0.5


def _lru_phases(x_ref, gate_ref, cw_ref, cb_ref, wg_ref, bg_ref, lam_ref, out_ref, xs, us, g_s, a_s, b_s, h_st,
                *, tt, conv_w, blk, pitch, n_chunks, gate_rows):
    hist = SUBLANES
    chunk = tt // n_chunks
    step = lambda ref, t: ref.at[pl.ds(t, SUBLANES, stride=pitch), :]
    slab = lambda n, off=0: slice(n * pitch + off, n * pitch + off + tt)
    st = {}

    def conv(c):
        if c == 0:
            for n in range(SUBLANES):
                xs[slab(n, hist), :] = x_ref[:, n * blk:(n + 1) * blk]
            st["taps"] = [cw_ref[k] for k in range(conv_w)]
            st["win"] = [step(xs, hist - (conv_w - 1) + k)[...] for k in range(conv_w - 1)]
        win = st["win"]
        for t in range(c * chunk, (c + 1) * chunk):
            win = win + [step(xs, hist + t)[...]]
            u = cb_ref[...]
            for k in range(conv_w):
                u = u + st["taps"][k] * win[k]
            step(us, t)[...] = u
            win = win[1:]
        st["win"] = win
        if c == n_chunks - 1:
            for n in range(SUBLANES):
                xs[n * pitch:n * pitch + hist, :] = xs[n * pitch + tt:n * pitch + tt + hist, :]

    def gate_dot(n):
        g_s[n] = _dot(us[slab(n), :].astype(BF16), wg_ref[n]) + bg_ref[n]

    def gate(p):
        n, r = divmod(p, tt // gate_rows)
        rows = slice(n * pitch + r * gate_rows, n * pitch + (r + 1) * gate_rows)
        un = us[rows, :]
        t = jnp.tanh(g_s[n, r * gate_rows:(r + 1) * gate_rows, :])
        c = (0.5 * LRU_C) * jax.nn.log_sigmoid(lam_ref[:, n * blk:(n + 1) * blk])
        log_a = c * t[:, :blk] + c
        a = jnp.exp(log_a)
        a_s[rows, :] = a
        y = -jnp.tanh(log_a) * (a * a + 1.0)
        mult = jnp.where(y > 0.0, y * lax.rsqrt(y), 0.0)
        hun = 0.5 * un
        b_s[rows, :] = mult * (hun * t[:, blk:] + hun)

    def scan(c):
        h = h_st[...] if c == 0 else st["h"]
        for t in range(c * chunk, (c + 1) * chunk):
            h = step(a_s, t)[...] * h + step(b_s, t)[...]
            step(b_s, t)[...] = h
        st["h"] = h
        if c == n_chunks - 1:
            h_st[...] = h

    def out(n):
        cols = slice(n * blk, (n + 1) * blk)
        x = gate_ref[:, cols]
        hx = 0.5 * x
        gelu = hx * jnp.tanh(x * (GELU_K + (GELU_K * 0.044715) * (x * x))) + hx
        out_ref[:, cols] = (b_s[slab(n), :] * gelu).astype(out_ref.dtype)

    bind = lambda f, count: [functools.partial(f, c) for c in range(count)]
    return (bind(conv, n_chunks), bind(gate_dot, SUBLANES), bind(gate, SUBLANES * (tt // gate_rows)),
            bind(scan, n_chunks), bind(out, SUBLANES))


def _mix_kernel(lx_ref, lg_ref, cw_ref, cb_ref, wg_ref, bg_ref, lam_ref,
                att_ref, ga_ref, gb_ref, x_ref, mod_ref, wa_ref, wb_ref, wo_ref, g2_ref,
                x1_ref, h2_ref, xs, us, g_s, a_s, b_s, h_st, yb_next, yb_cur, merged_scr,
                *, n_tiles, tiles_per_batch, lru, cw_a, cw_o):
    i = pl.program_id(0)
    d = x_ref.shape[1]

    @pl.when(i == 0)
    def _():
        yb_next[...] = jnp.zeros_like(yb_next)

    @pl.when(jnp.minimum(i, n_tiles - 1) % tiles_per_batch == 0)
    def _():
        for n in range(SUBLANES):
            xs[n * lru["pitch"]:n * lru["pitch"] + SUBLANES, :] = jnp.zeros((SUBLANES, xs.shape[1]), F32)
        h_st[...] = jnp.zeros_like(h_st)

    yb_cur[...] = yb_next[...]
    conv, gate_dot, gate, scan, out = _lru_phases(lx_ref, lg_ref, cw_ref, cb_ref, wg_ref, bg_ref, lam_ref, yb_next,
                                                  xs, us, g_s, a_s, b_s, h_st, **lru)

    def branches(c):
        cs = slice(c * cw_a, (c + 1) * cw_a)
        ya = _dot(att_ref[...], wa_ref[:, cs])
        yb = _dot(yb_cur[...], wb_ref[:, cs])
        merged_scr[:, cs] = (_sigmoid(ga_ref[:, cs]) * ya + _sigmoid(gb_ref[:, cs]) * yb).astype(BF16)

    sq = []

    def outproj(c):
        cs = slice(c * cw_o, (c + 1) * cw_o)
        x1 = x_ref[:, cs] + mod_ref[2:3, cs] * _dot(merged_scr[...], wo_ref[:, cs])
        x1_ref[:, cs] = x1
        sq.append(jnp.sum(x1 * x1, axis=-1, keepdims=True))

    def norm2():
        gain = g2_ref[...] * (1.0 + mod_ref[4:5, :])
        y = x1_ref[...] * lax.rsqrt(sum(sq) * (1.0 / d) + NORM_EPS)
        h2_ref[...] = (y * gain + mod_ref[3:4, :]).astype(BF16)

    first = [functools.partial(branches, c) for c in range(d // cw_a)]
    second = [functools.partial(outproj, c) for c in range(d // cw_o)]
    for f in conv + gate_dot + gate + scan + out + first + second:
        f()
    norm2()


def _mix(att, rest, x2, mod, conv_w, conv_b, wg, bg, lam, wa, wb, wo, g2, seq, tm=256):
    t, d = x2.shape
    k, w = conv_w.shape
    n_blocks, blk = wg.shape[0], wg.shape[1]
    assert n_blocks == SUBLANES and blk == HEAD_DIM, "one 128-lane channel slab per sublane"
    wa_k = wa.shape[0]
    n_tiles = t // tm
    tiles_per_batch = seq // tm
    pitch = tm + 2 * SUBLANES if (tm // SUBLANES) % 2 else tm + SUBLANES
    lru = dict(tt=tm, conv_w=k, blk=blk, pitch=pitch, n_chunks=1, gate_rows=tm)
    kern = functools.partial(_mix_kernel, n_tiles=n_tiles, tiles_per_batch=tiles_per_batch, lru=lru,
                             cw_a=d, cw_o=d)
    const = lambda shape: pl.BlockSpec(shape, lambda i: (0,) * len(shape), pipeline_mode=pl.Buffered(1))
    li = lambda i: jnp.minimum(i, n_tiles - 1)
    mi = lambda i: jnp.maximum(i - 1, 0)
    slab = lambda: pltpu.VMEM((SUBLANES * pitch, blk), F32)
    return pl.pallas_call(
        kern,
        grid=(n_tiles + 1,),
        in_specs=[pl.BlockSpec((tm, w), lambda i: (li(i), 0)),
                  pl.BlockSpec((tm, w), lambda i: (li(i), 1)),
                  const((k, SUBLANES, blk)), const((SUBLANES, blk)),
                  const((n_blocks, blk, 2 * blk)), const((n_blocks, 1, 2 * blk)), const((1, w)),
                  pl.BlockSpec((tm, wa_k), lambda i: (mi(i), 0)),
                  pl.BlockSpec((tm, d), lambda i: (mi(i), 1)),
                  pl.BlockSpec((tm, d), lambda i: (mi(i), 2)),
                  pl.BlockSpec((tm, d), lambda i: (mi(i), 0)),
                  pl.BlockSpec((None, N_MOD, d), lambda i: (mi(i) // tiles_per_batch, 0, 0)),
                  const((wa_k, d)), const((w, d)), const((d, d)), const((1, d))],
        out_specs=[pl.BlockSpec((tm, d), lambda i: (mi(i), 0)),
                   pl.BlockSpec((tm, d), lambda i: (mi(i), 0))],
        out_shape=[jax.ShapeDtypeStruct((t, d), F32), jax.ShapeDtypeStruct((t, d), BF16)],
        scratch_shapes=[slab(), slab(), pltpu.VMEM((n_blocks, tm, 2 * blk), F32), slab(), slab(),
                        pltpu.VMEM((SUBLANES, blk), F32),
                        pltpu.VMEM((tm, w), BF16), pltpu.VMEM((tm, w), BF16), pltpu.VMEM((tm, d), BF16)],
        compiler_params=_params(("arbitrary",)),
        name="mix",
    )(rest, rest, conv_w.reshape(k, SUBLANES, blk), conv_b.reshape(SUBLANES, blk), wg, bg, lam,
      att, rest, rest, x2, mod, wa, wb, wo, g2)


def _mlp_kernel(h2_ref, x1_ref, mod_ref, w1_ref, w2_ref, o_ref, acc_scr):
    f = pl.program_id(1)

    @pl.when(f == 0)
    def _():
        acc_scr[...] = jnp.zeros_like(acc_scr)

    a = jnp.maximum(_dot(h2_ref[...], w1_ref[...]), 0.0)
    acc_scr[...] += _dot((a * a).astype(BF16), w2_ref[...])

    @pl.when(f == pl.num_programs(1) - 1)
    def _():
        o_ref[...] = x1_ref[...] + mod_ref[5:6, :] * acc_scr[...]


def _mlp(h2, x1, mod, w1, w2, seq, tm=512, tf=1024):
    t, d = x1.shape
    dff = w1.shape[1]
    tiles_per_batch = seq // tm
    return pl.pallas_call(
        _mlp_kernel,
        grid=(t // tm, dff // tf),
        in_specs=[pl.BlockSpec((tm, d), lambda i, f: (i, 0)),
                  pl.BlockSpec((tm, d), lambda i, f: (i, 0)),
                  pl.BlockSpec((None, N_MOD, d), lambda i, f: (i // tiles_per_batch, 0, 0)),
                  pl.BlockSpec((d, tf), lambda i, f: (0, f)),
                  pl.BlockSpec((tf, d), lambda i, f: (f, 0))],
        out_specs=pl.BlockSpec((tm, d), lambda i, f: (i, 0)),
        out_shape=jax.ShapeDtypeStruct((t, d), F32),
        scratch_shapes=[pltpu.VMEM((tm, d), F32)],
        compiler_params=_params(("arbitrary", "arbitrary")),
        name="mlp",
    )(h2, x1, mod, w1, w2)


def kernel(x, c, w_ada, b_ada, norm1_g, w_in, b_in, q_norm_g, k_norm_g, conv_w, conv_b, w_rg, b_rg, w_ig, b_ig,
           lru_lambda, w_branch_a, w_branch_b, w_out, norm2_g, w_mlp_in, w_mlp_out):
    batch, seq, d = x.shape
    depth = w_ada.shape[0]
    n_blocks, blk = w_rg.shape[1], w_rg.shape[2]
    n_heads = w_branch_a.shape[1] // HEAD_DIM
    row = lambda v: v.reshape(1, -1)

    x2 = x.reshape(batch * seq, d)
    c_pad = jnp.zeros((SUBLANES, d), F32).at[:batch].set(c)
    for l in range(depth):
        mod = _ada(c_pad, w_ada[l], row(b_ada[l]))[:batch].reshape(batch, N_MOD, d)
        qkv, rest = _inproj(x2, mod, row(norm1_g[l]), w_in[l].astype(BF16), row(b_in[l]),
                            row(q_norm_g[l]), row(k_norm_g[l]), seq)
        att, (wo, w1, w2) = _attn(qkv, batch, seq, n_heads, (w_out[l], w_mlp_in[l], w_mlp_out[l]))
        wg = (0.5 * jnp.concatenate([w_rg[l], w_ig[l]], axis=-1)).astype(BF16)
        bg = 0.5 * jnp.concatenate([b_rg[l].reshape(n_blocks, 1, blk), b_ig[l].reshape(n_blocks, 1, blk)], axis=-1)
        x1, h2 = _mix(att, rest, x2, mod, conv_w[l], row(conv_b[l]), wg, bg, row(lru_lambda[l]),
                      w_branch_a[l].astype(BF16), w_branch_b[l].astype(BF16), wo, row(norm2_g[l]), seq)
        x2 = _mlp(h2, x1, mod, w1, w2, seq)
    return x2.reshape(batch, seq, d)
```

```python
import functools
import math

import jax
import jax.numpy as jnp
from jax import lax
from jax.experimental import pallas as pl
from jax.experimental.pallas import tpu as pltpu

F32 = jnp.float32
BF16 = jnp.bfloat16

NORM_EPS = 1e-6
LRU_C = 8.0
GELU_K = math.sqrt(2.0 / math.pi)
HEAD_DIM = 128
N_MOD = 6
SUBLANES = 8
F32_EXP_UNDERFLOW = 104.0
VMEM_LIMIT = 56 * 1024 * 1024
MLP_VMEM_LIMIT = 62 * 1024 * 1024


def _params(sem, vmem=VMEM_LIMIT):
    return pltpu.CompilerParams(dimension_semantics=sem, vmem_limit_bytes=vmem)


def _dot(a, b):
    return jnp.dot(a, b, preferred_element_type=F32)


def _ada_kernel(c_ref, w_ref, b_ref, o_ref):
    ca = jax.nn.silu(c_ref[...]).astype(BF16)
    o_ref[...] = _dot(ca, w_ref[...].astype(BF16)) + b_ref[...]


def _ada(c_pad, w_ada, b_ada, tn=1024):
    m, d = c_pad.shape
    n = w_ada.shape[1]
    return pl.pallas_call(
        _ada_kernel,
        grid=(n // tn,),
        in_specs=[pl.BlockSpec((m, d), lambda j: (0, 0)),
                  pl.BlockSpec((d, tn), lambda j: (0, j)),
                  pl.BlockSpec((1, tn), lambda j: (0, j))],
        out_specs=pl.BlockSpec((m, tn), lambda j: (0, j)),
        out_shape=jax.ShapeDtypeStruct((m, n), F32),
        compiler_params=_params(("arbitrary",)),
        name="ada",
    )(c_pad, w_ada, b_ada)


def _inproj_kernel(x_ref, mod_ref, g1_ref, w_ref, b_ref, qg_ref, kg_ref, obf_ref, of_ref, h_scr, *, n_heads, nf):
    j = pl.program_id(1)

    @pl.when(j == 0)
    def _():
        x = x_ref[...]
        ms = jnp.mean(x * x, axis=-1, keepdims=True)
        gain = g1_ref[...] * (1.0 + mod_ref[1:2, :])
        h_scr[...] = (x * lax.rsqrt(ms + NORM_EPS) * gain + mod_ref[0:1, :]).astype(BF16)

    of_ref[...] = _dot(h_scr[...], w_ref[...]) + b_ref[...]

    @pl.when(jnp.logical_and(j >= nf, j < nf + 2))
    def _():
        g = jnp.where(j == nf, qg_ref[...], kg_ref[...])
        for hh in range(n_heads):
            sl = slice(hh * HEAD_DIM, (hh + 1) * HEAD_DIM)
            a = of_ref[:, sl]
            ms = jnp.mean(a * a, axis=-1, keepdims=True)
            obf_ref[:, sl] = (a * lax.rsqrt(ms + NORM_EPS) * g).astype(BF16)

    @pl.when(j == nf + 2)
    def _():
        obf_ref[...] = of_ref[...].astype(BF16)


def _inproj(x2, mod, g1, w_in, b_in, qg, kg, seq, tm=1024, tn=1024):
    t, d = x2.shape
    n = w_in.shape[1]
    nj = n // tn
    nf = nj - 3
    tiles_per_batch = seq // tm
    kern = functools.partial(_inproj_kernel, n_heads=tn // HEAD_DIM, nf=nf)
    wcol = lambda j: (j + 3) % nj
    return pl.pallas_call(
        kern,
        grid=(t // tm, nj),
        in_specs=[pl.BlockSpec((tm, d), lambda i, j: (i, 0)),
                  pl.BlockSpec((None, N_MOD, d), lambda i, j: (i // tiles_per_batch, 0, 0)),
                  pl.BlockSpec((1, d), lambda i, j: (0, 0)),
                  pl.BlockSpec((d, tn), lambda i, j: (0, wcol(j))),
                  pl.BlockSpec((1, tn), lambda i, j: (0, wcol(j))),
                  pl.BlockSpec((1, HEAD_DIM), lambda i, j: (0, 0)),
                  pl.BlockSpec((1, HEAD_DIM), lambda i, j: (0, 0))],
        out_specs=[pl.BlockSpec((tm, tn), lambda i, j: (i, jnp.maximum(j - nf, 0))),
                   pl.BlockSpec((tm, tn), lambda i, j: (i, jnp.minimum(j, nf)))],
        out_shape=[jax.ShapeDtypeStruct((t, 3 * tn), BF16),
                   jax.ShapeDtypeStruct((t, (nf + 1) * tn), F32)],
        scratch_shapes=[pltpu.VMEM((tm, d), BF16)],
        compiler_params=_params(("arbitrary", "arbitrary")),
        name="inproj",
    )(x2, mod, g1, w_in, b_in, qg, kg)


def _cumsum_matrix(width):
    j = jnp.arange(width)
    m = jnp.concatenate([(j[:, None] >= j[None, :]).astype(BF16), jnp.ones((width, HEAD_DIM), BF16)], axis=1)
    return jnp.concatenate([m, m], axis=0)


def _attn_kernel(q_ref, k_ref, v_ref, m_ref, *refs, n_cast, qb, hb, first_fast, scale):
    cast_in, o_ref, cast_out = refs[:n_cast], refs[n_cast], refs[n_cast + 1:2 * n_cast + 1]
    carry_scr, acc_scr = refs[2 * n_cast + 1:]
    for src, dst in zip(cast_in, cast_out):
        dst[...] = src[...].astype(dst.dtype)

    i = pl.program_id(2)
    blk = HEAD_DIM
    row = lax.broadcasted_iota(jnp.int32, (blk, blk), 0)
    col = lax.broadcasted_iota(jnp.int32, (blk, blk), 1)
    causal = col < row

    def tiles(jobs, nb, diag, first):
        sl = lambda n: slice(n * blk, (n + 1) * blk)
        zs = [lax.dot_general(q_ref[sl(s), sl(h)], k_ref[pl.ds(kstart, nb * blk), sl(h)],
                              (((1,), (1,)), ((), ())), preferred_element_type=F32) * scale
              for s, h, kstart in jobs]
        lhs = []
        for z in zs:
            lom = -(jnp.maximum(z, 0.0) + jnp.log(1.0 + jnp.exp(-jnp.abs(z))))
            row_lhs = []
            for c in range(nb):
                lom_c = jnp.where(causal, lom[:, sl(c)], 0.0) if (diag and c == nb - 1) else lom[:, sl(c)]
                hi = lom_c.astype(BF16)
                lo = (lom_c - hi.astype(F32)).astype(BF16)
                row_lhs.append(jnp.concatenate([hi, lo], axis=1))
            lhs.append(row_lhs)
        boths = [[_dot(a, m_ref[...]) for a in row_lhs] for row_lhs in lhs]
        ws = []
        for (s, h, _), z, both in zip(jobs, zs, boths):
            carry = None if first else carry_scr[s, h]
            w = [None] * nb
            for c in reversed(range(nb)):
                logw = z[:, sl(c)] + both[c][:, :blk]
                if carry is not None:
                    logw = logw + carry
                e = jnp.exp(logw)
                w[c] = (jnp.where(causal, e, 0.0) if (diag and c == nb - 1) else e).astype(BF16)
                carry = both[c][:, blk:] if carry is None else carry + both[c][:, blk:]
            carry_scr[s, h] = carry
            ws.append(w[0] if nb == 1 else jnp.concatenate(w, axis=1))
        for (s, h, kstart), w in zip(jobs, ws):
            pv = _dot(w, v_ref[pl.ds(kstart, nb * blk), sl(h)])
            acc_scr[s, h] = pv if first else acc_scr[s, h] + pv

    g0 = i * qb
    fast = i >= first_fast

    def jobs_for(s, block):
        kstart = pl.multiple_of(block * blk, blk)
        return [(s, h, kstart) for h in range(hb)]

    @pl.when(fast)
    def _():
        tiles([job for s in range(qb) for job in jobs_for(s, g0 + s - 2)], 3, True, True)

    @pl.when(jnp.logical_not(fast))
    def _():
        tiles([job for s in range(qb) for job in jobs_for(s, g0 + s)], 1, True, True)

    jbase = g0 - jnp.where(fast, 3, 1)
    nmax = jbase + qb - 1

    def cond(st):
        n, mx = st
        return jnp.logical_and(n <= nmax, mx > -F32_EXP_UNDERFLOW)

    def body(st):
        n, _ = st
        for s in range(qb):
            j = jbase + s - n

            @pl.when(j >= 0)
            def _():
                tiles(jobs_for(s, j), 1, False, False)

        return n + 1, jnp.max(carry_scr[...])

    lax.while_loop(cond, body, (jnp.int32(0), jnp.max(carry_scr[...])))
    for s in range(qb):
        for h in range(hb):
            o_ref[s * blk:(s + 1) * blk, h * blk:(h + 1) * blk] = acc_scr[s, h].astype(o_ref.dtype)


def _attn(qkv, batch, seq, n_heads, to_bf16=(), qb=4, hb=4):
    t = qkv.shape[0]
    blk = HEAD_DIM
    tq = qb * blk
    nq = seq // tq
    ng = n_heads // hb
    n_steps = batch * ng * nq
    first_fast = -(-2 // qb)
    kern = functools.partial(_attn_kernel, n_cast=len(to_bf16), qb=qb, hb=hb, first_fast=first_fast,
                             scale=1.0 / math.sqrt(HEAD_DIM))
    once = pl.Buffered(1)
    m = _cumsum_matrix(blk)
    bf16_rows = 2 * SUBLANES
    for w in to_bf16:
        assert w.shape[0] % (n_steps * bf16_rows) == 0, "row slices must be whole bf16 tiles"
    slices = [pl.BlockSpec((w.shape[0] // n_steps, w.shape[1]), lambda b, g, i: ((b * ng + g) * nq + i, 0))
              for w in to_bf16]
    att, *copies = pl.pallas_call(
        kern,
        grid=(batch, ng, nq),
        in_specs=[pl.BlockSpec((tq, hb * blk), lambda b, g, i: (b * nq + i, g)),
                  pl.BlockSpec((seq, hb * blk), lambda b, g, i: (b, ng + g), pipeline_mode=once),
                  pl.BlockSpec((seq, hb * blk), lambda b, g, i: (b, 2 * ng + g), pipeline_mode=once),
                  pl.BlockSpec(m.shape, lambda b, g, i: (0, 0), pipeline_mode=once)] + slices,
        out_specs=[pl.BlockSpec((tq, hb * blk), lambda b, g, i: (b * nq + i, g))] + slices,
        out_shape=[jax.ShapeDtypeStruct((t, n_heads * blk), BF16)]
        + [jax.ShapeDtypeStruct(w.shape, BF16) for w in to_bf16],
        scratch_shapes=[pltpu.VMEM((qb, hb, blk, blk), F32), pltpu.VMEM((qb, hb, blk, blk), F32)],
        compiler_params=_params(("arbitrary", "arbitrary", "arbitrary")),
        name="attn",
    )(qkv, qkv, qkv, m, *to_bf16)
    return att, copies


def _sigmoid(x):
    return 0.5 * jnp.tanh(0.5 * x) + 0.5


def _lru_phases(x_ref, gate_ref, cw_ref, cb_ref, wg_ref, bg_ref, lam_ref, out_ref, xs, us, g_s, a_s, b_s, h_st,
                *, tt, conv_w, blk, pitch, n_chunks, gate_rows):
    hist = SUBLANES
    chunk = tt // n_chunks
    step = lambda ref, t: ref.at[pl.ds(t, SUBLANES, stride=pitch), :]
    slab = lambda n, off=0: slice(n * pitch + off, n * pitch + off + tt)
    st = {}

    def conv(c):
        if c == 0:
            for n in range(SUBLANES):
                xs[slab(n, hist), :] = x_ref[:, n * blk:(n + 1) * blk]
            st["taps"] = [cw_ref[k] for k in range(conv_w)]
            st["win"] = [step(xs, hist - (conv_w - 1) + k)[...] for k in range(conv_w - 1)]
        win = st["win"]
        for t in range(c * chunk, (c + 1) * chunk):
            win = win + [step(xs, hist + t)[...]]
            u = cb_ref[...]
            for k in range(conv_w):
                u = u + st["taps"][k] * win[k]
            step(us, t)[...] = u
            win = win[1:]
        st["win"] = win
        if c == n_chunks - 1:
            for n in range(SUBLANES):
                xs[n * pitch:n * pitch + hist, :] = xs[n * pitch + tt:n * pitch + tt + hist, :]

    def gate_dot(n):
        g_s[n] = _dot(us[slab(n), :].astype(BF16), wg_ref[n]) + bg_ref[n]

    def gate(p):
        n, r = divmod(p, tt // gate_rows)
        rows = slice(n * pitch + r * gate_rows, n * pitch + (r + 1) * gate_rows)
        un = us[rows, :]
        t = jnp.tanh(g_s[n, r * gate_rows:(r + 1) * gate_rows, :])
        c = (0.5 * LRU_C) * jax.nn.log_sigmoid(lam_ref[:, n * blk:(n + 1) * blk])
        log_a = c * t[:, :blk] + c
        a = jnp.exp(log_a)
        a_s[rows, :] = a
        y = -jnp.tanh(log_a) * (a * a + 1.0)
        mult = jnp.where(y > 0.0, y * lax.rsqrt(y), 0.0)
        hun = 0.5 * un
        b_s[rows, :] = mult * (hun * t[:, blk:] + hun)

    def scan(c):
        h = h_st[...] if c == 0 else st["h"]
        for t in range(c * chunk, (c + 1) * chunk):
            h = step(a_s, t)[...] * h + step(b_s, t)[...]
            step(b_s, t)[...] = h
        st["h"] = h
        if c == n_chunks - 1:
            h_st[...] = h

    def out(n):
        cols = slice(n * blk, (n + 1) * blk)
        x = gate_ref[:, cols]
        hx = 0.5 * x
        gelu = hx * jnp.tanh(x * (GELU_K + (GELU_K * 0.044715) * (x * x))) + hx
        out_ref[:, cols] = (b_s[slab(n), :] * gelu).astype(out_ref.dtype)

    bind = lambda f, count: [functools.partial(f, c) for c in range(count)]
    return (bind(conv, n_chunks), bind(gate_dot, SUBLANES), bind(gate, SUBLANES * (tt // gate_rows)),
            bind(scan, n_chunks), bind(out, SUBLANES))


def _mix_kernel(lx_ref, lg_ref, cw_ref, cb_ref, wg_ref, bg_ref, lam_ref,
                att_ref, ga_ref, gb_ref, x_ref, mod_ref, wa_ref, wb_ref, wo_ref, g2_ref,
                x1_ref, h2_ref, xs, us, g_s, a_s, b_s, h_st, yb_next, yb_cur, merged_scr,
                *, n_tiles, tiles_per_batch, lru, cw_a, cw_o):
    i = pl.program_id(0)
    d = x_ref.shape[1]

    @pl.when(i == 0)
    def _():
        yb_next[...] = jnp.zeros_like(yb_next)

    @pl.when(jnp.minimum(i, n_tiles - 1) % tiles_per_batch == 0)
    def _():
        for n in range(SUBLANES):
            xs[n * lru["pitch"]:n * lru["pitch"] + SUBLANES, :] = jnp.zeros((SUBLANES, xs.shape[1]), F32)
        h_st[...] = jnp.zeros_like(h_st)

    yb_cur[...] = yb_next[...]
    conv, gate_dot, gate, scan, out = _lru_phases(lx_ref, lg_ref, cw_ref, cb_ref, wg_ref, bg_ref, lam_ref, yb_next,
                                                  xs, us, g_s, a_s, b_s, h_st, **lru)

    def branches(c):
        cs = slice(c * cw_a, (c + 1) * cw_a)
        ya = _dot(att_ref[...], wa_ref[:, cs])
        yb = _dot(yb_cur[...], wb_ref[:, cs])
        merged_scr[:, cs] = (_sigmoid(ga_ref[:, cs]) * ya + _sigmoid(gb_ref[:, cs]) * yb).astype(BF16)

    sq = []

    def outproj(c):
        cs = slice(c * cw_o, (c + 1) * cw_o)
        x1 = x_ref[:, cs] + mod_ref[2:3, cs] * _dot(merged_scr[...], wo_ref[:, cs])
        x1_ref[:, cs] = x1
        sq.append(jnp.sum(x1 * x1, axis=-1, keepdims=True))

    def norm2():
        gain = g2_ref[...] * (1.0 + mod_ref[4:5, :])
        y = x1_ref[...] * lax.rsqrt(sum(sq) * (1.0 / d) + NORM_EPS)
        h2_ref[...] = (y * gain + mod_ref[3:4, :]).astype(BF16)

    first = [functools.partial(branches, c) for c in range(d // cw_a)]
    second = [functools.partial(outproj, c) for c in range(d // cw_o)]
    for f in conv + gate_dot + gate + scan + out + first + second:
        f()
    norm2()


def _mix(att, rest, x2, mod, conv_w, conv_b, wg, bg, lam, wa, wb, wo, g2, seq, tm=256):
    t, d = x2.shape
    k, w = conv_w.shape
    n_blocks, blk = wg.shape[0], wg.shape[1]
    assert n_blocks == SUBLANES and blk == HEAD_DIM, "one 128-lane channel slab per sublane"
    wa_k = wa.shape[0]
    n_tiles = t // tm
    tiles_per_batch = seq // tm
    pitch = tm + 2 * SUBLANES if (tm // SUBLANES) % 2 else tm + SUBLANES
    lru = dict(tt=tm, conv_w=k, blk=blk, pitch=pitch, n_chunks=1, gate_rows=tm)
    kern = functools.partial(_mix_kernel, n_tiles=n_tiles, tiles_per_batch=tiles_per_batch, lru=lru,
                             cw_a=d, cw_o=d)
    const = lambda shape: pl.BlockSpec(shape, lambda i: (0,) * len(shape), pipeline_mode=pl.Buffered(1))
    li = lambda i: jnp.minimum(i, n_tiles - 1)
    mi = lambda i: jnp.maximum(i - 1, 0)
    slab = lambda: pltpu.VMEM((SUBLANES * pitch, blk), F32)
    return pl.pallas_call(
        kern,
        grid=(n_tiles + 1,),
        in_specs=[pl.BlockSpec((tm, w), lambda i: (li(i), 0)),
                  pl.BlockSpec((tm, w), lambda i: (li(i), 1)),
                  const((k, SUBLANES, blk)), const((SUBLANES, blk)),
                  const((n_blocks, blk, 2 * blk)), const((n_blocks, 1, 2 * blk)), const((1, w)),
                  pl.BlockSpec((tm, wa_k), lambda i: (mi(i), 0)),
                  pl.BlockSpec((tm, d), lambda i: (mi(i), 1)),
                  pl.BlockSpec((tm, d), lambda i: (mi(i), 2)),
                  pl.BlockSpec((tm, d), lambda i: (mi(i), 0)),
                  pl.BlockSpec((None, N_MOD, d), lambda i: (mi(i) // tiles_per_batch, 0, 0)),
                  const((wa_k, d)), const((w, d)), const((d, d)), const((1, d))],
        out_specs=[pl.BlockSpec((tm, d), lambda i: (mi(i), 0)),
                   pl.BlockSpec((tm, d), lambda i: (mi(i), 0))],
        out_shape=[jax.ShapeDtypeStruct((t, d), F32), jax.ShapeDtypeStruct((t, d), BF16)],
        scratch_shapes=[slab(), slab(), pltpu.VMEM((n_blocks, tm, 2 * blk), F32), slab(), slab(),
                        pltpu.VMEM((SUBLANES, blk), F32),
                        pltpu.VMEM((tm, w), BF16), pltpu.VMEM((tm, w), BF16), pltpu.VMEM((tm, d), BF16)],
        compiler_params=_params(("arbitrary",)),
        name="mix",
    )(rest, rest, conv_w.reshape(k, SUBLANES, blk), conv_b.reshape(SUBLANES, blk), wg, bg, lam,
      att, rest, rest, x2, mod, wa, wb, wo, g2)


def _mlp_kernel(h2_ref, x1_ref, mod_ref, w1_ref, w2_ref, o_ref, acc_scr, *, sub):
    f = pl.program_id(1)

    @pl.when(f == 0)
    def _():
        acc_scr[...] = jnp.zeros_like(acc_scr)

    for c in range(w1_ref.shape[1] // sub):
        cs = slice(c * sub, (c + 1) * sub)
        a = jnp.maximum(_dot(h2_ref[...], w1_ref[:, cs]), 0.0)
        acc_scr[...] += _dot((a * a).astype(BF16), w2_ref[cs, :])

    @pl.when(f == pl.num_programs(1) - 1)
    def _():
        o_ref[...] = x1_ref[...] + mod_ref[5:6, :] * acc_scr[...]


def _mlp(h2, x1, mod, w1, w2, seq, tm=512, tf=2048, sub=1024):
    t, d = x1.shape
    dff = w1.shape[1]
    tiles_per_batch = seq // tm
    return pl.pallas_call(
        functools.partial(_mlp_kernel, sub=sub),
        grid=(t // tm, dff // tf),
        in_specs=[pl.BlockSpec((tm, d), lambda i, f: (i, 0)),
                  pl.BlockSpec((tm, d), lambda i, f: (i, 0)),
                  pl.BlockSpec((None, N_MOD, d), lambda i, f: (i // tiles_per_batch, 0, 0)),
                  pl.BlockSpec((d, tf), lambda i, f: (0, f)),
                  pl.BlockSpec((tf, d), lambda i, f: (f, 0))],
        out_specs=pl.BlockSpec((tm, d), lambda i, f: (i, 0)),
        out_shape=jax.ShapeDtypeStruct((t, d), F32),
        scratch_shapes=[pltpu.VMEM((tm, d), F32)],
        compiler_params=_params(("arbitrary", "arbitrary"), vmem=MLP_VMEM_LIMIT),
        name="mlp",
    )(h2, x1, mod, w1, w2)


def kernel(x, c, w_ada, b_ada, norm1_g, w_in, b_in, q_norm_g, k_norm_g, conv_w, conv_b, w_rg, b_rg, w_ig, b_ig,
           lru_lambda, w_branch_a, w_branch_b, w_out, norm2_g, w_mlp_in, w_mlp_out):
    batch, seq, d = x.shape
    depth = w_ada.shape[0]
    n_blocks, blk = w_rg.shape[1], w_rg.shape[2]
    n_heads = w_branch_a.shape[1] // HEAD_DIM
    row = lambda v: v.reshape(1, -1)

    x2 = x.reshape(batch * seq, d)
    c_pad = jnp.zeros((SUBLANES, d), F32).at[:batch].set(c)
    for l in range(depth):
        mod = _ada(c_pad, w_ada[l], row(b_ada[l]))[:batch].reshape(batch, N_MOD, d)
        qkv, rest = _inproj(x2, mod, row(norm1_g[l]), w_in[l].astype(BF16), row(b_in[l]),
                            row(q_norm_g[l]), row(k_norm_g[l]), seq)
        att, (wo, w1, w2) = _attn(qkv, batch, seq, n_heads, (w_out[l], w_mlp_in[l], w_mlp_out[l]))
        wg = (0.5 * jnp.concatenate([w_rg[l], w_ig[l]], axis=-1)).astype(BF16)
        bg = 0.5 * jnp.concatenate([b_rg[l].reshape(n_blocks, 1, blk), b_ig[l].reshape(n_blocks, 1, blk)], axis=-1)
        x1, h2 = _mix(att, rest, x2, mod, conv_w[l], row(conv_b[l]), wg, bg, row(lru_lambda[l]),
                      w_branch_a[l].astype(BF16), w_branch_b[l].astype(BF16), wo, row(norm2_g[l]), seq)
        x2 = _mlp(h2, x1, mod, w1, w2, seq)
    return x2.reshape(batch, seq, d)
```

```python
import functools
import math

import jax
import jax.numpy as jnp
from jax import lax
from jax.experimental import pallas as pl
from jax.experimental.pallas import tpu as pltpu

F32 = jnp.float32
BF16 = jnp.bfloat16

NORM_EPS = 1e-6
LRU_C = 8.0
GELU_K = math.sqrt(2.0 / math.pi)
HEAD_DIM = 128
N_MOD = 6
SUBLANES = 8
F32_EXP_UNDERFLOW = 104.0
VMEM_LIMIT = 56 * 1024 * 1024
MLP_VMEM_LIMIT = 62 * 1024 * 1024


def _params(sem, vmem=VMEM_LIMIT):
    return pltpu.CompilerParams(dimension_semantics=sem, vmem_limit_bytes=vmem)


def _dot(a, b):
    return jnp.dot(a, b, preferred_element_type=F32)


def _ada_kernel(c_ref, w_ref, b_ref, o_ref):
    ca = jax.nn.silu(c_ref[...]).astype(BF16)
    o_ref[...] = _dot(ca, w_ref[...].astype(BF16)) + b_ref[...]


def _ada(c_pad, w_ada, b_ada, tn=1024):
    m, d = c_pad.shape
    n = w_ada.shape[1]
    return pl.pallas_call(
        _ada_kernel,
        grid=(n // tn,),
        in_specs=[pl.BlockSpec((m, d), lambda j: (0, 0)),
                  pl.BlockSpec((d, tn), lambda j: (0, j)),
                  pl.BlockSpec((1, tn), lambda j: (0, j))],
        out_specs=pl.BlockSpec((m, tn), lambda j: (0, j)),
        out_shape=jax.ShapeDtypeStruct((m, n), F32),
        compiler_params=_params(("arbitrary",)),
        name="ada",
    )(c_pad, w_ada, b_ada)


def _inproj_kernel(x_ref, mod_ref, g1_ref, w_ref, b_ref, qg_ref, kg_ref, obf_ref, of_ref, h_scr, *, nf, kinds):
    j = pl.program_id(1)

    @pl.when(j == 0)
    def _():
        x = x_ref[...]
        ms = jnp.mean(x * x, axis=-1, keepdims=True)
        gain = g1_ref[...] * (1.0 + mod_ref[1:2, :])
        h_scr[...] = (x * lax.rsqrt(ms + NORM_EPS) * gain + mod_ref[0:1, :]).astype(BF16)

    of_ref[...] = _dot(h_scr[...], w_ref[...]) + b_ref[...]

    for b, tile_kinds in enumerate(kinds):
        @pl.when(j == nf + b)
        def _():
            for hh, kind in enumerate(tile_kinds):
                sl = slice(hh * HEAD_DIM, (hh + 1) * HEAD_DIM)
                a = of_ref[:, sl]
                if kind != "v":
                    ms = jnp.mean(a * a, axis=-1, keepdims=True)
                    a = a * lax.rsqrt(ms + NORM_EPS) * (qg_ref if kind == "q" else kg_ref)[...]
                obf_ref[:, sl] = a.astype(BF16)


def _inproj(x2, mod, g1, w_in, b_in, qg, kg, seq, sb_width, tm=1024, tn=1536):
    t, d = x2.shape
    n = w_in.shape[1]
    nb = 3 * sb_width // tn
    nf = n // tn - nb
    assert nb * tn == 3 * sb_width and (nf + nb) * tn == n and tn % HEAD_DIM == 0
    heads = [kind for kind in "qkv" for _ in range(sb_width // HEAD_DIM)]
    per_tile = tn // HEAD_DIM
    kinds = tuple(tuple(heads[b * per_tile:(b + 1) * per_tile]) for b in range(nb))
    tiles_per_batch = seq // tm
    kern = functools.partial(_inproj_kernel, nf=nf, kinds=kinds)
    wcol = lambda j: (j + nb) % (nf + nb)
    return pl.pallas_call(
        kern,
        grid=(t // tm, nf + nb),
        in_specs=[pl.BlockSpec((tm, d), lambda i, j: (i, 0)),
                  pl.BlockSpec((None, N_MOD, d), lambda i, j: (i // tiles_per_batch, 0, 0)),
                  pl.BlockSpec((1, d), lambda i, j: (0, 0)),
                  pl.BlockSpec((d, tn), lambda i, j: (0, wcol(j))),
                  pl.BlockSpec((1, tn), lambda i, j: (0, wcol(j))),
                  pl.BlockSpec((1, HEAD_DIM), lambda i, j: (0, 0)),
                  pl.BlockSpec((1, HEAD_DIM), lambda i, j: (0, 0))],
        out_specs=[pl.BlockSpec((tm, tn), lambda i, j: (i, jnp.maximum(j - nf, 0))),
                   pl.BlockSpec((tm, tn), lambda i, j: (i, jnp.minimum(j, nf)))],
        out_shape=[jax.ShapeDtypeStruct((t, nb * tn), BF16),
                   jax.ShapeDtypeStruct((t, (nf + 1) * tn), F32)],
        scratch_shapes=[pltpu.VMEM((tm, d), BF16)],
        compiler_params=_params(("arbitrary", "arbitrary")),
        name="inproj",
    )(x2, mod, g1, w_in, b_in, qg, kg)


def _cumsum_matrix(width):
    j = jnp.arange(width)
    m = jnp.concatenate([(j[:, None] >= j[None, :]).astype(BF16), jnp.ones((width, HEAD_DIM), BF16)], axis=1)
    return jnp.concatenate([m, m], axis=0)


def _attn_kernel(q_ref, k_ref, v_ref, m_ref, *refs, n_cast, qb, hb, first_fast, scale):
    cast_in, o_ref, cast_out = refs[:n_cast], refs[n_cast], refs[n_cast + 1:2 * n_cast + 1]
    carry_scr, acc_scr = refs[2 * n_cast + 1:]
    for src, dst in zip(cast_in, cast_out):
        dst[...] = src[...].astype(dst.dtype)

    i = pl.program_id(2)
    blk = HEAD_DIM
    row = lax.broadcasted_iota(jnp.int32, (blk, blk), 0)
    col = lax.broadcasted_iota(jnp.int32, (blk, blk), 1)
    causal = col < row

    def tiles(jobs, nb, diag, first):
        sl = lambda n: slice(n * blk, (n + 1) * blk)
        zs = [lax.dot_general(q_ref[sl(s), sl(h)], k_ref[pl.ds(kstart, nb * blk), sl(h)],
                              (((1,), (1,)), ((), ())), preferred_element_type=F32) * scale
              for s, h, kstart in jobs]
        lhs = []
        for z in zs:
            lom = -(jnp.maximum(z, 0.0) + jnp.log(1.0 + jnp.exp(-jnp.abs(z))))
            row_lhs = []
            for c in range(nb):
                lom_c = jnp.where(causal, lom[:, sl(c)], 0.0) if (diag and c == nb - 1) else lom[:, sl(c)]
                hi = lom_c.astype(BF16)
                lo = (lom_c - hi.astype(F32)).astype(BF16)
                row_lhs.append(jnp.concatenate([hi, lo], axis=1))
            lhs.append(row_lhs)
        boths = [[_dot(a, m_ref[...]) for a in row_lhs] for row_lhs in lhs]
        ws = []
        for (s, h, _), z, both in zip(jobs, zs, boths):
            carry = None if first else carry_scr[s, h]
            w = [None] * nb
            for c in reversed(range(nb)):
                logw = z[:, sl(c)] + both[c][:, :blk]
                if carry is not None:
                    logw = logw + carry
                e = jnp.exp(logw)
                w[c] = (jnp.where(causal, e, 0.0) if (diag and c == nb - 1) else e).astype(BF16)
                carry = both[c][:, blk:] if carry is None else carry + both[c][:, blk:]
            carry_scr[s, h] = carry
            ws.append(w[0] if nb == 1 else jnp.concatenate(w, axis=1))
        for (s, h, kstart), w in zip(jobs, ws):
            pv = _dot(w, v_ref[pl.ds(kstart, nb * blk), sl(h)])
            acc_scr[s, h] = pv if first else acc_scr[s, h] + pv

    g0 = i * qb
    fast = i >= first_fast

    def jobs_for(s, block):
        kstart = pl.multiple_of(block * blk, blk)
        return [(s, h, kstart) for h in range(hb)]

    @pl.when(fast)
    def _():
        tiles([job for s in range(qb) for job in jobs_for(s, g0 + s - 2)], 3, True, True)

    @pl.when(jnp.logical_not(fast))
    def _():
        tiles([job for s in range(qb) for job in jobs_for(s, g0 + s)], 1, True, True)

    jbase = g0 - jnp.where(fast, 3, 1)
    nmax = jbase + qb - 1

    def cond(st):
        n, mx = st
        return jnp.logical_and(n <= nmax, mx > -F32_EXP_UNDERFLOW)

    def body(st):
        n, _ = st
        for s in range(qb):
            j = jbase + s - n

            @pl.when(j >= 0)
            def _():
                tiles(jobs_for(s, j), 1, False, False)

        return n + 1, jnp.max(carry_scr[...])

    lax.while_loop(cond, body, (jnp.int32(0), jnp.max(carry_scr[...])))
    for s in range(qb):
        for h in range(hb):
            o_ref[s * blk:(s + 1) * blk, h * blk:(h + 1) * blk] = acc_scr[s, h].astype(o_ref.dtype)


def _attn(qkv, batch, seq, n_heads, to_bf16=(), qb=4, hb=4):
    t = qkv.shape[0]
    blk = HEAD_DIM
    tq = qb * blk
    nq = seq // tq
    ng = n_heads // hb
    n_steps = batch * ng * nq
    first_fast = -(-2 // qb)
    kern = functools.partial(_attn_kernel, n_cast=len(to_bf16), qb=qb, hb=hb, first_fast=first_fast,
                             scale=1.0 / math.sqrt(HEAD_DIM))
    once = pl.Buffered(1)
    m = _cumsum_matrix(blk)
    bf16_rows = 2 * SUBLANES
    for w in to_bf16:
        assert w.shape[0] % (n_steps * bf16_rows) == 0, "row slices must be whole bf16 tiles"
    slices = [pl.BlockSpec((w.shape[0] // n_steps, w.shape[1]), lambda b, g, i: ((b * ng + g) * nq + i, 0))
              for w in to_bf16]
    att, *copies = pl.pallas_call(
        kern,
        grid=(batch, ng, nq),
        in_specs=[pl.BlockSpec((tq, hb * blk), lambda b, g, i: (b * nq + i, g)),
                  pl.BlockSpec((seq, hb * blk), lambda b, g, i: (b, ng + g), pipeline_mode=once),
                  pl.BlockSpec((seq, hb * blk), lambda b, g, i: (b, 2 * ng + g), pipeline_mode=once),
                  pl.BlockSpec(m.shape, lambda b, g, i: (0, 0), pipeline_mode=once)] + slices,
        out_specs=[pl.BlockSpec((tq, hb * blk), lambda b, g, i: (b * nq + i, g))] + slices,
        out_shape=[jax.ShapeDtypeStruct((t, n_heads * blk), BF16)]
        + [jax.ShapeDtypeStruct(w.shape, BF16) for w in to_bf16],
        scratch_shapes=[pltpu.VMEM((qb, hb, blk, blk), F32), pltpu.VMEM((qb, hb, blk, blk), F32)],
        compiler_params=_params(("arbitrary", "arbitrary", "arbitrary")),
        name="attn",
    )(qkv, qkv, qkv, m, *to_bf16)
    return att, copies


def _sigmoid(x):
    return 0.5 * jnp.tanh(0.5 * x) + 0.5


def _lru_phases(x_ref, gate_ref, cw_ref, cb_ref, wg_ref, bg_ref, lam_ref, out_ref, xs, us, g_s, a_s, b_s, h_st,
                *, tt, conv_w, blk, pitch, n_chunks, gate_rows):
    hist = SUBLANES
    chunk = tt // n_chunks
    step = lambda ref, t: ref.at[pl.ds(t, SUBLANES, stride=pitch), :]
    slab = lambda n, off=0: slice(n * pitch + off, n * pitch + off + tt)
    st = {}

    def conv(c):
        if c == 0:
            for n in range(SUBLANES):
                xs[slab(n, hist), :] = x_ref[:, n * blk:(n + 1) * blk]
            st["taps"] = [cw_ref[k] for k in range(conv_w)]
            st["win"] = [step(xs, hist - (conv_w - 1) + k)[...] for k in range(conv_w - 1)]
        win = st["win"]
        for t in range(c * chunk, (c + 1) * chunk):
            win = win + [step(xs, hist + t)[...]]
            u = cb_ref[...]
            for k in range(conv_w):
                u = u + st["taps"][k] * win[k]
            step(us, t)[...] = u
            win = win[1:]
        st["win"] = win
        if c == n_chunks - 1:
            for n in range(SUBLANES):
                xs[n * pitch:n * pitch + hist, :] = xs[n * pitch + tt:n * pitch + tt + hist, :]

    def gate_dot(n):
        g_s[n] = _dot(us[slab(n), :].astype(BF16), wg_ref[n]) + bg_ref[n]

    def gate(p):
        n, r = divmod(p, tt // gate_rows)
        rows = slice(n * pitch + r * gate_rows, n * pitch + (r + 1) * gate_rows)
        un = us[rows, :]
        t = jnp.tanh(g_s[n, r * gate_rows:(r + 1) * gate_rows, :])
        c = (0.5 * LRU_C) * jax.nn.log_sigmoid(lam_ref[:, n * blk:(n + 1) * blk])
        log_a = c * t[:, :blk] + c
        a = jnp.exp(log_a)
        a_s[rows, :] = a
        y = -jnp.tanh(log_a) * (a * a + 1.0)
        mult = jnp.where(y > 0.0, y * lax.rsqrt(y), 0.0)
        hun = 0.5 * un
        b_s[rows, :] = mult * (hun * t[:, blk:] + hun)

    def scan(c):
        h = h_st[...] if c == 0 else st["h"]
        for t in range(c * chunk, (c + 1) * chunk):
            h = step(a_s, t)[...] * h + step(b_s, t)[...]
            step(b_s, t)[...] = h
        st["h"] = h
        if c == n_chunks - 1:
            h_st[...] = h

    def out(n):
        cols = slice(n * blk, (n + 1) * blk)
        x = gate_ref[:, cols]
        hx = 0.5 * x
        gelu = hx * jnp.tanh(x * (GELU_K + (GELU_K * 0.044715) * (x * x))) + hx
        out_ref[:, cols] = (b_s[slab(n), :] * gelu).astype(out_ref.dtype)

    bind = lambda f, count: [functools.partial(f, c) for c in range(count)]
    return (bind(conv, n_chunks), bind(gate_dot, SUBLANES), bind(gate, SUBLANES * (tt // gate_rows)),
            bind(scan, n_chunks), bind(out, SUBLANES))


def _mix_kernel(lx_ref, lg_ref, cw_ref, cb_ref, wg_ref, bg_ref, lam_ref,
                att_ref, ga_ref, gb_ref, x_ref, mod_ref, wa_ref, wb_ref, wo_ref, g2_ref,
                x1_ref, h2_ref, xs, us, g_s, a_s, b_s, h_st, yb_next, yb_cur, merged_scr,
                *, n_tiles, tiles_per_batch, lru, cw_a, cw_o):
    i = pl.program_id(0)
    d = x_ref.shape[1]

    @pl.when(i == 0)
    def _():
        yb_next[...] = jnp.zeros_like(yb_next)

    @pl.when(jnp.minimum(i, n_tiles - 1) % tiles_per_batch == 0)
    def _():
        for n in range(SUBLANES):
            xs[n * lru["pitch"]:n * lru["pitch"] + SUBLANES, :] = jnp.zeros((SUBLANES, xs.shape[1]), F32)
        h_st[...] = jnp.zeros_like(h_st)

    yb_cur[...] = yb_next[...]
    conv, gate_dot, gate, scan, out = _lru_phases(lx_ref, lg_ref, cw_ref, cb_ref, wg_ref, bg_ref, lam_ref, yb_next,
                                                  xs, us, g_s, a_s, b_s, h_st, **lru)

    def branches(c):
        cs = slice(c * cw_a, (c + 1) * cw_a)
        ya = _dot(att_ref[...], wa_ref[:, cs])
        yb = _dot(yb_cur[...], wb_ref[:, cs])
        merged_scr[:, cs] = (_sigmoid(ga_ref[:, cs]) * ya + _sigmoid(gb_ref[:, cs]) * yb).astype(BF16)

    sq = []

    def outproj(c):
        cs = slice(c * cw_o, (c + 1) * cw_o)
        x1 = x_ref[:, cs] + mod_ref[2:3, cs] * _dot(merged_scr[...], wo_ref[:, cs])
        x1_ref[:, cs] = x1
        sq.append(jnp.sum(x1 * x1, axis=-1, keepdims=True))

    def norm2():
        gain = g2_ref[...] * (1.0 + mod_ref[4:5, :])
        y = x1_ref[...] * lax.rsqrt(sum(sq) * (1.0 / d) + NORM_EPS)
        h2_ref[...] = (y * gain + mod_ref[3:4, :]).astype(BF16)

    first = [functools.partial(branches, c) for c in range(d // cw_a)]
    second = [functools.partial(outproj, c) for c in range(d // cw_o)]
    for f in conv + gate_dot + gate + scan + out + first + second:
        f()
    norm2()


def _mix(att, rest, x2, mod, conv_w, conv_b, wg, bg, lam, wa, wb, wo, g2, seq, tm=256):
    t, d = x2.shape
    k, w = conv_w.shape
    n_blocks, blk = wg.shape[0], wg.shape[1]
    assert n_blocks == SUBLANES and blk == HEAD_DIM, "one 128-lane channel slab per sublane"
    wa_k = wa.shape[0]
    n_tiles = t // tm
    tiles_per_batch = seq // tm
    pitch = tm + 2 * SUBLANES if (tm // SUBLANES) % 2 else tm + SUBLANES
    lru = dict(tt=tm, conv_w=k, blk=blk, pitch=pitch, n_chunks=1, gate_rows=tm)
    kern = functools.partial(_mix_kernel, n_tiles=n_tiles, tiles_per_batch=tiles_per_batch, lru=lru,
                             cw_a=d, cw_o=d)
    const = lambda shape: pl.BlockSpec(shape, lambda i: (0,) * len(shape), pipeline_mode=pl.Buffered(1))
    li = lambda i: jnp.minimum(i, n_tiles - 1)
    mi = lambda i: jnp.maximum(i - 1, 0)
    slab = lambda: pltpu.VMEM((SUBLANES * pitch, blk), F32)
    return pl.pallas_call(
        kern,
        grid=(n_tiles + 1,),
        in_specs=[pl.BlockSpec((tm, w), lambda i: (li(i), 0)),
                  pl.BlockSpec((tm, w), lambda i: (li(i), 1)),
                  const((k, SUBLANES, blk)), const((SUBLANES, blk)),
                  const((n_blocks, blk, 2 * blk)), const((n_blocks, 1, 2 * blk)), const((1, w)),
                  pl.BlockSpec((tm, wa_k), lambda i: (mi(i), 0)),
                  pl.BlockSpec((tm, d), lambda i: (mi(i), 1)),
                  pl.BlockSpec((tm, d), lambda i: (mi(i), 2)),
                  pl.BlockSpec((tm, d), lambda i: (mi(i), 0)),
                  pl.BlockSpec((None, N_MOD, d), lambda i: (mi(i) // tiles_per_batch, 0, 0)),
                  const((wa_k, d)), const((w, d)), const((d, d)), const((1, d))],
        out_specs=[pl.BlockSpec((tm, d), lambda i: (mi(i), 0)),
                   pl.BlockSpec((tm, d), lambda i: (mi(i), 0))],
        out_shape=[jax.ShapeDtypeStruct((t, d), F32), jax.ShapeDtypeStruct((t, d), BF16)],
        scratch_shapes=[slab(), slab(), pltpu.VMEM((n_blocks, tm, 2 * blk), F32), slab(), slab(),
                        pltpu.VMEM((SUBLANES, blk), F32),
                        pltpu.VMEM((tm, w), BF16), pltpu.VMEM((tm, w), BF16), pltpu.VMEM((tm, d), BF16)],
        compiler_params=_params(("arbitrary",)),
        name="mix",
    )(rest, rest, conv_w.reshape(k, SUBLANES, blk), conv_b.reshape(SUBLANES, blk), wg, bg, lam,
      att, rest, rest, x2, mod, wa, wb, wo, g2)


def _mlp_kernel(h2_ref, x1_ref, mod_ref, w1_ref, w2_ref, o_ref, acc_scr, *, sub):
    f = pl.program_id(1)

    @pl.when(f == 0)
    def _():
        acc_scr[...] = jnp.zeros_like(acc_scr)

    for c in range(w1_ref.shape[1] // sub):
        cs = slice(c * sub, (c + 1) * sub)
        a = jnp.maximum(_dot(h2_ref[...], w1_ref[:, cs]), 0.0)
        acc_scr[...] += _dot((a * a).astype(BF16), w2_ref[cs, :])

    @pl.when(f == pl.num_programs(1) - 1)
    def _():
        o_ref[...] = x1_ref[...] + mod_ref[5:6, :] * acc_scr[...]


def _mlp(h2, x1, mod, w1, w2, seq, tm=512, tf=2048, sub=1024):
    t, d = x1.shape
    dff = w1.shape[1]
    tiles_per_batch = seq // tm
    return pl.pallas_call(
        functools.partial(_mlp_kernel, sub=sub),
        grid=(t // tm, dff // tf),
        in_specs=[pl.BlockSpec((tm, d), lambda i, f: (i, 0)),
                  pl.BlockSpec((tm, d), lambda i, f: (i, 0)),
                  pl.BlockSpec((None, N_MOD, d), lambda i, f: (i // tiles_per_batch, 0, 0)),
                  pl.BlockSpec((d, tf), lambda i, f: (0, f)),
                  pl.BlockSpec((tf, d), lambda i, f: (f, 0))],
        out_specs=pl.BlockSpec((tm, d), lambda i, f: (i, 0)),
        out_shape=jax.ShapeDtypeStruct((t, d), F32),
        scratch_shapes=[pltpu.VMEM((tm, d), F32)],
        compiler_params=_params(("arbitrary", "arbitrary"), vmem=MLP_VMEM_LIMIT),
        name="mlp",
    )(h2, x1, mod, w1, w2)


def kernel(x, c, w_ada, b_ada, norm1_g, w_in, b_in, q_norm_g, k_norm_g, conv_w, conv_b, w_rg, b_rg, w_ig, b_ig,
           lru_lambda, w_branch_a, w_branch_b, w_out, norm2_g, w_mlp_in, w_mlp_out):
    batch, seq, d = x.shape
    depth = w_ada.shape[0]
    n_blocks, blk = w_rg.shape[1], w_rg.shape[2]
    n_heads = w_branch_a.shape[1] // HEAD_DIM
    row = lambda v: v.reshape(1, -1)

    x2 = x.reshape(batch * seq, d)
    c_pad = jnp.zeros((SUBLANES, d), F32).at[:batch].set(c)
    for l in range(depth):
        mod = _ada(c_pad, w_ada[l], row(b_ada[l]))[:batch].reshape(batch, N_MOD, d)
        qkv, rest = _inproj(x2, mod, row(norm1_g[l]), w_in[l].astype(BF16), row(b_in[l]),
                            row(q_norm_g[l]), row(k_norm_g[l]), seq, n_heads * HEAD_DIM)
        att, (wo, w1, w2) = _attn(qkv, batch, seq, n_heads, (w_out[l], w_mlp_in[l], w_mlp_out[l]))
        wg = (0.5 * jnp.concatenate([w_rg[l], w_ig[l]], axis=-1)).astype(BF16)
        bg = 0.5 * jnp.concatenate([b_rg[l].reshape(n_blocks, 1, blk), b_ig[l].reshape(n_blocks, 1, blk)], axis=-1)
        x1, h2 = _mix(att, rest, x2, mod, conv_w[l], row(conv_b[l]), wg, bg, row(lru_lambda[l]),
                      w_branch_a[l].astype(BF16), w_branch_b[l].astype(BF16), wo, row(norm2_g[l]), seq)
        x2 = _mlp(h2, x1, mod, w1, w2, seq)
    return x2.reshape(batch, seq, d)
```

```python
import functools
import math

import jax
import jax.numpy as jnp
from jax import lax
from jax.experimental import pallas as pl
from jax.experimental.pallas import tpu as pltpu

F32 = jnp.float32
BF16 = jnp.bfloat16

NORM_EPS = 1e-6
LRU_C = 8.0
GELU_K = math.sqrt(2.0 / math.pi)
HEAD_DIM = 128
N_MOD = 6
SUBLANES = 8
F32_EXP_UNDERFLOW = 104.0
VMEM_LIMIT = 56 * 1024 * 1024
MLP_VMEM_LIMIT = 62 * 1024 * 1024


def _params(sem, vmem=VMEM_LIMIT):
    return pltpu.CompilerParams(dimension_semantics=sem, vmem_limit_bytes=vmem)


def _dot(a, b):
    return jnp.dot(a, b, preferred_element_type=F32)


def _ada_kernel(c_ref, w_ref, b_ref, o_ref):
    ca = jax.nn.silu(c_ref[...]).astype(BF16)
    o_ref[...] = _dot(ca, w_ref[...].astype(BF16)) + b_ref[...]


def _ada(c_pad, w_ada, b_ada, tn=1024):
    m, d = c_pad.shape
    n = w_ada.shape[1]
    return pl.pallas_call(
        _ada_kernel,
        grid=(n // tn,),
        in_specs=[pl.BlockSpec((m, d), lambda j: (0, 0)),
                  pl.BlockSpec((d, tn), lambda j: (0, j)),
                  pl.BlockSpec((1, tn), lambda j: (0, j))],
        out_specs=pl.BlockSpec((m, tn), lambda j: (0, j)),
        out_shape=jax.ShapeDtypeStruct((m, n), F32),
        compiler_params=_params(("arbitrary",)),
        name="ada",
    )(c_pad, w_ada, b_ada)


def _inproj_kernel(x_ref, mod_ref, g1_ref, w_ref, b_ref, qg_ref, kg_ref, obf_ref, of_ref, h_scr, *, nf, kinds):
    j = pl.program_id(1)

    @pl.when(j == 0)
    def _():
        x = x_ref[...]
        ms = jnp.mean(x * x, axis=-1, keepdims=True)
        gain = g1_ref[...] * (1.0 + mod_ref[1:2, :])
        h_scr[...] = (x * lax.rsqrt(ms + NORM_EPS) * gain + mod_ref[0:1, :]).astype(BF16)

    of_ref[...] = _dot(h_scr[...], w_ref[...]) + b_ref[...]

    for b, tile_kinds in enumerate(kinds):
        @pl.when(j == nf + b)
        def _():
            for hh, kind in enumerate(tile_kinds):
                sl = slice(hh * HEAD_DIM, (hh + 1) * HEAD_DIM)
                a = of_ref[:, sl]
                if kind != "v":
                    ms = jnp.mean(a * a, axis=-1, keepdims=True)
                    a = a * lax.rsqrt(ms + NORM_EPS) * (qg_ref if kind == "q" else kg_ref)[...]
                obf_ref[:, sl] = a.astype(BF16)


def _inproj(x2, mod, g1, w_in, b_in, qg, kg, seq, sb_width, tm=1024, tn=1536):
    t, d = x2.shape
    n = w_in.shape[1]
    nb = 3 * sb_width // tn
    nf = n // tn - nb
    assert nb * tn == 3 * sb_width and (nf + nb) * tn == n and tn % HEAD_DIM == 0
    heads = [kind for kind in "qkv" for _ in range(sb_width // HEAD_DIM)]
    per_tile = tn // HEAD_DIM
    kinds = tuple(tuple(heads[b * per_tile:(b + 1) * per_tile]) for b in range(nb))
    tiles_per_batch = seq // tm
    kern = functools.partial(_inproj_kernel, nf=nf, kinds=kinds)
    wcol = lambda j: (j + nb) % (nf + nb)
    return pl.pallas_call(
        kern,
        grid=(t // tm, nf + nb),
        in_specs=[pl.BlockSpec((tm, d), lambda i, j: (i, 0)),
                  pl.BlockSpec((None, N_MOD, d), lambda i, j: (i // tiles_per_batch, 0, 0)),
                  pl.BlockSpec((1, d), lambda i, j: (0, 0)),
                  pl.BlockSpec((d, tn), lambda i, j: (0, wcol(j))),
                  pl.BlockSpec((1, tn), lambda i, j: (0, wcol(j))),
                  pl.BlockSpec((1, HEAD_DIM), lambda i, j: (0, 0)),
                  pl.BlockSpec((1, HEAD_DIM), lambda i, j: (0, 0))],
        out_specs=[pl.BlockSpec((tm, tn), lambda i, j: (i, jnp.maximum(j - nf, 0))),
                   pl.BlockSpec((tm, tn), lambda i, j: (i, jnp.minimum(j, nf)))],
        out_shape=[jax.ShapeDtypeStruct((t, nb * tn), BF16),
                   jax.ShapeDtypeStruct((t, (nf + 1) * tn), F32)],
        scratch_shapes=[pltpu.VMEM((tm, d), BF16)],
        compiler_params=_params(("arbitrary", "arbitrary")),
        name="inproj",
    )(x2, mod, g1, w_in, b_in, qg, kg)


def _cumsum_matrix(width):
    j = jnp.arange(width)
    m = jnp.concatenate([(j[:, None] >= j[None, :]).astype(BF16), jnp.ones((width, HEAD_DIM), BF16)], axis=1)
    return jnp.concatenate([m, m], axis=0)


def _attn_kernel(q_ref, k_ref, v_ref, m_ref, *refs, n_cast, qb, hb, first_fast, scale):
    cast_in, o_ref, cast_out = refs[:n_cast], refs[n_cast], refs[n_cast + 1:2 * n_cast + 1]
    carry_scr, acc_scr = refs[2 * n_cast + 1:]
    for src, dst in zip(cast_in, cast_out):
        dst[...] = src[...].astype(dst.dtype)

    i = pl.program_id(2)
    blk = HEAD_DIM
    row = lax.broadcasted_iota(jnp.int32, (blk, blk), 0)
    col = lax.broadcasted_iota(jnp.int32, (blk, blk), 1)
    causal = col < row

    def tiles(jobs, nb, diag, first):
        sl = lambda n: slice(n * blk, (n + 1) * blk)
        zs = [lax.dot_general(q_ref[sl(s), sl(h)], k_ref[pl.ds(kstart, nb * blk), sl(h)],
                              (((1,), (1,)), ((), ())), preferred_element_type=F32) * scale
              for s, h, kstart in jobs]
        lhs = []
        for z in zs:
            lom = -(jnp.maximum(z, 0.0) + jnp.log(1.0 + jnp.exp(-jnp.abs(z))))
            row_lhs = []
            for c in range(nb):
                lom_c = jnp.where(causal, lom[:, sl(c)], 0.0) if (diag and c == nb - 1) else lom[:, sl(c)]
                hi = lom_c.astype(BF16)
                lo = (lom_c - hi.astype(F32)).astype(BF16)
                row_lhs.append(jnp.concatenate([hi, lo], axis=1))
            lhs.append(row_lhs)
        boths = [[_dot(a, m_ref[...]) for a in row_lhs] for row_lhs in lhs]
        ws = []
        for (s, h, _), z, both in zip(jobs, zs, boths):
            carry = None if first else carry_scr[s, h]
            w = [None] * nb
            for c in reversed(range(nb)):
                logw = z[:, sl(c)] + both[c][:, :blk]
                if carry is not None:
                    logw = logw + carry
                e = jnp.exp(logw)
                w[c] = (jnp.where(causal, e, 0.0) if (diag and c == nb - 1) else e).astype(BF16)
                carry = both[c][:, blk:] if carry is None else carry + both[c][:, blk:]
            carry_scr[s, h] = carry
            ws.append(w[0] if nb == 1 else jnp.concatenate(w, axis=1))
        for (s, h, kstart), w in zip(jobs, ws):
            pv = _dot(w, v_ref[pl.ds(kstart, nb * blk), sl(h)])
            acc_scr[s, h] = pv if first else acc_scr[s, h] + pv

    g0 = i * qb
    group = min(qb, 4)
    fast = i >= first_fast

    def jobs_for(s, block):
        kstart = pl.multiple_of(block * blk, blk)
        return [(s, h, kstart) for h in range(hb)]

    @pl.when(fast)
    def _():
        for s0 in range(0, qb, group):
            tiles([job for s in range(s0, s0 + group) for job in jobs_for(s, g0 + s - 2)], 3, True, True)

    @pl.when(jnp.logical_not(fast))
    def _():
        for s0 in range(0, qb, group):
            tiles([job for s in range(s0, s0 + group) for job in jobs_for(s, g0 + s)], 1, True, True)

    jbase = g0 - jnp.where(fast, 3, 1)
    nmax = jbase + qb - 1

    def cond(st):
        n, mx = st
        return jnp.logical_and(n <= nmax, mx > -F32_EXP_UNDERFLOW)

    def body(st):
        n, _ = st
        for s in range(qb):
            j = jbase + s - n

            @pl.when(j >= 0)
            def _():
                tiles(jobs_for(s, j), 1, False, False)

        return n + 1, jnp.max(carry_scr[...])

    lax.while_loop(cond, body, (jnp.int32(0), jnp.max(carry_scr[...])))
    for s in range(qb):
        for h in range(hb):
            o_ref[s * blk:(s + 1) * blk, h * blk:(h + 1) * blk] = acc_scr[s, h].astype(o_ref.dtype)


def _attn(qkv, batch, seq, n_heads, to_bf16=(), qb=8, hb=4):
    t = qkv.shape[0]
    blk = HEAD_DIM
    tq = qb * blk
    nq = seq // tq
    ng = n_heads // hb
    n_steps = batch * ng * nq
    first_fast = -(-2 // qb)
    kern = functools.partial(_attn_kernel, n_cast=len(to_bf16), qb=qb, hb=hb, first_fast=first_fast,
                             scale=1.0 / math.sqrt(HEAD_DIM))
    once = pl.Buffered(1)
    m = _cumsum_matrix(blk)
    bf16_rows = 2 * SUBLANES
    for w in to_bf16:
        assert w.shape[0] % (n_steps * bf16_rows) == 0, "row slices must be whole bf16 tiles"
    slices = [pl.BlockSpec((w.shape[0] // n_steps, w.shape[1]), lambda b, g, i: ((b * ng + g) * nq + i, 0))
              for w in to_bf16]
    att, *copies = pl.pallas_call(
        kern,
        grid=(batch, ng, nq),
        in_specs=[pl.BlockSpec((tq, hb * blk), lambda b, g, i: (b * nq + i, g)),
                  pl.BlockSpec((seq, hb * blk), lambda b, g, i: (b, ng + g), pipeline_mode=once),
                  pl.BlockSpec((seq, hb * blk), lambda b, g, i: (b, 2 * ng + g), pipeline_mode=once),
                  pl.BlockSpec(m.shape, lambda b, g, i: (0, 0), pipeline_mode=once)] + slices,
        out_specs=[pl.BlockSpec((tq, hb * blk), lambda b, g, i: (b * nq + i, g))] + slices,
        out_shape=[jax.ShapeDtypeStruct((t, n_heads * blk), BF16)]
        + [jax.ShapeDtypeStruct(w.shape, BF16) for w in to_bf16],
        scratch_shapes=[pltpu.VMEM((qb, hb, blk, blk), F32), pltpu.VMEM((qb, hb, blk, blk), F32)],
        compiler_params=_params(("arbitrary", "arbitrary", "arbitrary")),
        name="attn",
    )(qkv, qkv, qkv, m, *to_bf16)
    return att, copies


def _sigmoid(x):
    return 0.5 * jnp.tanh(0.5 * x) + 0.5


def _lru_phases(x_ref, gate_ref, cw_ref, cb_ref, wg_ref, bg_ref, lam_ref, out_ref, xs, us, g_s, a_s, b_s, h_st,
                *, tt, conv_w, blk, pitch, n_chunks, gate_rows):
    hist = SUBLANES
    chunk = tt // n_chunks
    step = lambda ref, t: ref.at[pl.ds(t, SUBLANES, stride=pitch), :]
    slab = lambda n, off=0: slice(n * pitch + off, n * pitch + off + tt)
    st = {}

    def conv(c):
        if c == 0:
            for n in range(SUBLANES):
                xs[slab(n, hist), :] = x_ref[:, n * blk:(n + 1) * blk]
            st["taps"] = [cw_ref[k] for k in range(conv_w)]
            st["win"] = [step(xs, hist - (conv_w - 1) + k)[...] for k in range(conv_w - 1)]
        win = st["win"]
        for t in range(c * chunk, (c + 1) * chunk):
            win = win + [step(xs, hist + t)[...]]
            u = cb_ref[...]
            for k in range(conv_w):
                u = u + st["taps"][k] * win[k]
            step(us, t)[...] = u
            win = win[1:]
        st["win"] = win
        if c == n_chunks - 1:
            for n in range(SUBLANES):
                xs[n * pitch:n * pitch + hist, :] = xs[n * pitch + tt:n * pitch + tt + hist, :]

    def gate_dot(n):
        g_s[n] = _dot(us[slab(n), :].astype(BF16), wg_ref[n]) + bg_ref[n]

    def gate(p):
        n, r = divmod(p, tt // gate_rows)
        rows = slice(n * pitch + r * gate_rows, n * pitch + (r + 1) * gate_rows)
        un = us[rows, :]
        t = jnp.tanh(g_s[n, r * gate_rows:(r + 1) * gate_rows, :])
        c = (0.5 * LRU_C) * jax.nn.log_sigmoid(lam_ref[:, n * blk:(n + 1) * blk])
        log_a = c * t[:, :blk] + c
        a = jnp.exp(log_a)
        a_s[rows, :] = a
        y = -jnp.tanh(log_a) * (a * a + 1.0)
        mult = jnp.where(y > 0.0, y * lax.rsqrt(y), 0.0)
        hun = 0.5 * un
        b_s[rows, :] = mult * (hun * t[:, blk:] + hun)

    def scan(c):
        h = h_st[...] if c == 0 else st["h"]
        for t in range(c * chunk, (c + 1) * chunk):
            h = step(a_s, t)[...] * h + step(b_s, t)[...]
            step(b_s, t)[...] = h
        st["h"] = h
        if c == n_chunks - 1:
            h_st[...] = h

    def out(n):
        cols = slice(n * blk, (n + 1) * blk)
        x = gate_ref[:, cols]
        hx = 0.5 * x
        gelu = hx * jnp.tanh(x * (GELU_K + (GELU_K * 0.044715) * (x * x))) + hx
        out_ref[:, cols] = (b_s[slab(n), :] * gelu).astype(out_ref.dtype)

    bind = lambda f, count: [functools.partial(f, c) for c in range(count)]
    return (bind(conv, n_chunks), bind(gate_dot, SUBLANES), bind(gate, SUBLANES * (tt // gate_rows)),
            bind(scan, n_chunks), bind(out, SUBLANES))


def _mix_kernel(lx_ref, lg_ref, cw_ref, cb_ref, wg_ref, bg_ref, lam_ref,
                att_ref, ga_ref, gb_ref, x_ref, mod_ref, wa_ref, wb_ref, wo_ref, g2_ref,
                x1_ref, h2_ref, xs, us, g_s, a_s, b_s, h_st, yb_next, yb_cur, merged_scr,
                *, n_tiles, tiles_per_batch, lru, cw_a, cw_o):
    i = pl.program_id(0)
    d = x_ref.shape[1]

    @pl.when(i == 0)
    def _():
        yb_next[...] = jnp.zeros_like(yb_next)

    @pl.when(jnp.minimum(i, n_tiles - 1) % tiles_per_batch == 0)
    def _():
        for n in range(SUBLANES):
            xs[n * lru["pitch"]:n * lru["pitch"] + SUBLANES, :] = jnp.zeros((SUBLANES, xs.shape[1]), F32)
        h_st[...] = jnp.zeros_like(h_st)

    yb_cur[...] = yb_next[...]
    conv, gate_dot, gate, scan, out = _lru_phases(lx_ref, lg_ref, cw_ref, cb_ref, wg_ref, bg_ref, lam_ref, yb_next,
                                                  xs, us, g_s, a_s, b_s, h_st, **lru)

    def branches(c):
        cs = slice(c * cw_a, (c + 1) * cw_a)
        ya = _dot(att_ref[...], wa_ref[:, cs])
        yb = _dot(yb_cur[...], wb_ref[:, cs])
        merged_scr[:, cs] = (_sigmoid(ga_ref[:, cs]) * ya + _sigmoid(gb_ref[:, cs]) * yb).astype(BF16)

    sq = []

    def outproj(c):
        cs = slice(c * cw_o, (c + 1) * cw_o)
        x1 = x_ref[:, cs] + mod_ref[2:3, cs] * _dot(merged_scr[...], wo_ref[:, cs])
        x1_ref[:, cs] = x1
        sq.append(jnp.sum(x1 * x1, axis=-1, keepdims=True))

    def norm2():
        gain = g2_ref[...] * (1.0 + mod_ref[4:5, :])
        y = x1_ref[...] * lax.rsqrt(sum(sq) * (1.0 / d) + NORM_EPS)
        h2_ref[...] = (y * gain + mod_ref[3:4, :]).astype(BF16)

    first = [functools.partial(branches, c) for c in range(d // cw_a)]
    second = [functools.partial(outproj, c) for c in range(d // cw_o)]
    for f in conv + gate_dot + gate + scan + out + first + second:
        f()
    norm2()


def _mix(att, rest, x2, mod, conv_w, conv_b, wg, bg, lam, wa, wb, wo, g2, seq, tm=256):
    t, d = x2.shape
    k, w = conv_w.shape
    n_blocks, blk = wg.shape[0], wg.shape[1]
    assert n_blocks == SUBLANES and blk == HEAD_DIM, "one 128-lane channel slab per sublane"
    wa_k = wa.shape[0]
    n_tiles = t // tm
    tiles_per_batch = seq // tm
    pitch = tm + 2 * SUBLANES if (tm // SUBLANES) % 2 else tm + SUBLANES
    lru = dict(tt=tm, conv_w=k, blk=blk, pitch=pitch, n_chunks=1, gate_rows=tm)
    kern = functools.partial(_mix_kernel, n_tiles=n_tiles, tiles_per_batch=tiles_per_batch, lru=lru,
                             cw_a=d, cw_o=d)
    const = lambda shape: pl.BlockSpec(shape, lambda i: (0,) * len(shape), pipeline_mode=pl.Buffered(1))
    li = lambda i: jnp.minimum(i, n_tiles - 1)
    mi = lambda i: jnp.maximum(i - 1, 0)
    slab = lambda: pltpu.VMEM((SUBLANES * pitch, blk), F32)
    return pl.pallas_call(
        kern,
        grid=(n_tiles + 1,),
        in_specs=[pl.BlockSpec((tm, w), lambda i: (li(i), 0)),
                  pl.BlockSpec((tm, w), lambda i: (li(i), 1)),
                  const((k, SUBLANES, blk)), const((SUBLANES, blk)),
                  const((n_blocks, blk, 2 * blk)), const((n_blocks, 1, 2 * blk)), const((1, w)),
                  pl.BlockSpec((tm, wa_k), lambda i: (mi(i), 0)),
                  pl.BlockSpec((tm, d), lambda i: (mi(i), 1)),
                  pl.BlockSpec((tm, d), lambda i: (mi(i), 2)),
                  pl.BlockSpec((tm, d), lambda i: (mi(i), 0)),
                  pl.BlockSpec((None, N_MOD, d), lambda i: (mi(i) // tiles_per_batch, 0, 0)),
                  const((wa_k, d)), const((w, d)), const((d, d)), const((1, d))],
        out_specs=[pl.BlockSpec((tm, d), lambda i: (mi(i), 0)),
                   pl.BlockSpec((tm, d), lambda i: (mi(i), 0))],
        out_shape=[jax.ShapeDtypeStruct((t, d), F32), jax.ShapeDtypeStruct((t, d), BF16)],
        scratch_shapes=[slab(), slab(), pltpu.VMEM((n_blocks, tm, 2 * blk), F32), slab(), slab(),
                        pltpu.VMEM((SUBLANES, blk), F32),
                        pltpu.VMEM((tm, w), BF16), pltpu.VMEM((tm, w), BF16), pltpu.VMEM((tm, d), BF16)],
        compiler_params=_params(("arbitrary",)),
        name="mix",
    )(rest, rest, conv_w.reshape(k, SUBLANES, blk), conv_b.reshape(SUBLANES, blk), wg, bg, lam,
      att, rest, rest, x2, mod, wa, wb, wo, g2)


def _mlp_kernel(h2_ref, x1_ref, mod_ref, w1_ref, w2_ref, o_ref, acc_scr, *, sub):
    f = pl.program_id(1)

    @pl.when(f == 0)
    def _():
        acc_scr[...] = jnp.zeros_like(acc_scr)

    for c in range(w1_ref.shape[1] // sub):
        cs = slice(c * sub, (c + 1) * sub)
        a = jnp.maximum(_dot(h2_ref[...], w1_ref[:, cs]), 0.0)
        acc_scr[...] += _dot((a * a).astype(BF16), w2_ref[cs, :])

    @pl.when(f == pl.num_programs(1) - 1)
    def _():
        o_ref[...] = x1_ref[...] + mod_ref[5:6, :] * acc_scr[...]


def _mlp(h2, x1, mod, w1, w2, seq, tm=512, tf=2048, sub=1024):
    t, d = x1.shape
    dff = w1.shape[1]
    tiles_per_batch = seq // tm
    return pl.pallas_call(
        functools.partial(_mlp_kernel, sub=sub),
        grid=(t // tm, dff // tf),
        in_specs=[pl.BlockSpec((tm, d), lambda i, f: (i, 0)),
                  pl.BlockSpec((tm, d), lambda i, f: (i, 0)),
                  pl.BlockSpec((None, N_MOD, d), lambda i, f: (i // tiles_per_batch, 0, 0)),
                  pl.BlockSpec((d, tf), lambda i, f: (0, f)),
                  pl.BlockSpec((tf, d), lambda i, f: (f, 0))],
        out_specs=pl.BlockSpec((tm, d), lambda i, f: (i, 0)),
        out_shape=jax.ShapeDtypeStruct((t, d), F32),
        scratch_shapes=[pltpu.VMEM((tm, d), F32)],
        compiler_params=_params(("arbitrary", "arbitrary"), vmem=MLP_VMEM_LIMIT),
        name="mlp",
    )(h2, x1, mod, w1, w2)


def kernel(x, c, w_ada, b_ada, norm1_g, w_in, b_in, q_norm_g, k_norm_g, conv_w, conv_b, w_rg, b_rg, w_ig, b_ig,
           lru_lambda, w_branch_a, w_branch_b, w_out, norm2_g, w_mlp_in, w_mlp_out):
    batch, seq, d = x.shape
    depth = w_ada.shape[0]
    n_blocks, blk = w_rg.shape[1], w_rg.shape[2]
    n_heads = w_branch_a.shape[1] // HEAD_DIM
    row = lambda v: v.reshape(1, -1)

    x2 = x.reshape(batch * seq, d)
    c_pad = jnp.zeros((SUBLANES, d), F32).at[:batch].set(c)
    for l in range(depth):
        mod = _ada(c_pad, w_ada[l], row(b_ada[l]))[:batch].reshape(batch, N_MOD, d)
        qkv, rest = _inproj(x2, mod, row(norm1_g[l]), w_in[l].astype(BF16), row(b_in[l]),
                            row(q_norm_g[l]), row(k_norm_g[l]), seq, n_heads * HEAD_DIM)
        att, (wo, w1, w2) = _attn(qkv, batch, seq, n_heads, (w_out[l], w_mlp_in[l], w_mlp_out[l]))
        wg = (0.5 * jnp.concatenate([w_rg[l], w_ig[l]], axis=-1)).astype(BF16)
        bg = 0.5 * jnp.concatenate([b_rg[l].reshape(n_blocks, 1, blk), b_ig[l].reshape(n_blocks, 1, blk)], axis=-1)
        x1, h2 = _mix(att, rest, x2, mod, conv_w[l], row(conv_b[l]), wg, bg, row(lru_lambda[l]),
                      w_branch_a[l].astype(BF16), w_branch_b[l].astype(BF16), wo, row(norm2_g[l]), seq)
        x2 = _mlp(h2, x1, mod, w1, w2, seq)
    return x2.reshape(batch, seq, d)
```

```python
import functools
import math

import jax
import jax.numpy as jnp
from jax import lax
from jax.experimental import pallas as pl
from jax.experimental.pallas import tpu as pltpu

F32 = jnp.float32
BF16 = jnp.bfloat16

NORM_EPS = 1e-6
LRU_C = 8.0
GELU_K = math.sqrt(2.0 / math.pi)
HEAD_DIM = 128
N_MOD = 6
SUBLANES = 8
F32_EXP_UNDERFLOW = 104.0
VMEM_LIMIT = 56 * 1024 * 1024
MLP_VMEM_LIMIT = 62 * 1024 * 1024


def _params(sem, vmem=VMEM_LIMIT):
    return pltpu.CompilerParams(dimension_semantics=sem, vmem_limit_bytes=vmem)


def _dot(a, b):
    return jnp.dot(a, b, preferred_element_type=F32)


def _ada_kernel(c_ref, w_ref, b_ref, o_ref):
    ca = jax.nn.silu(c_ref[...]).astype(BF16)
    o_ref[...] = _dot(ca, w_ref[...].astype(BF16)) + b_ref[...]


def _ada(c_pad, w_ada, b_ada, tn=1024):
    m, d = c_pad.shape
    n = w_ada.shape[1]
    return pl.pallas_call(
        _ada_kernel,
        grid=(n // tn,),
        in_specs=[pl.BlockSpec((m, d), lambda j: (0, 0)),
                  pl.BlockSpec((d, tn), lambda j: (0, j)),
                  pl.BlockSpec((1, tn), lambda j: (0, j))],
        out_specs=pl.BlockSpec((m, tn), lambda j: (0, j)),
        out_shape=jax.ShapeDtypeStruct((m, n), F32),
        compiler_params=_params(("arbitrary",)),
        name="ada",
    )(c_pad, w_ada, b_ada)


def _inproj_kernel(x_ref, mod_ref, g1_ref, w_ref, b_ref, qg_ref, kg_ref, obf_ref, of_ref, h_scr, *, nf, kinds):
    j = pl.program_id(1)

    @pl.when(j == 0)
    def _():
        x = x_ref[...]
        ms = jnp.mean(x * x, axis=-1, keepdims=True)
        gain = g1_ref[...] * (1.0 + mod_ref[1:2, :])
        h_scr[...] = (x * lax.rsqrt(ms + NORM_EPS) * gain + mod_ref[0:1, :]).astype(BF16)

    of_ref[...] = _dot(h_scr[...], w_ref[...]) + b_ref[...]

    for b, tile_kinds in enumerate(kinds):
        @pl.when(j == nf + b)
        def _():
            for hh, kind in enumerate(tile_kinds):
                sl = slice(hh * HEAD_DIM, (hh + 1) * HEAD_DIM)
                a = of_ref[:, sl]
                if kind != "v":
                    ms = jnp.mean(a * a, axis=-1, keepdims=True)
                    a = a * lax.rsqrt(ms + NORM_EPS) * (qg_ref if kind == "q" else kg_ref)[...]
                obf_ref[:, sl] = a.astype(BF16)


def _inproj(x2, mod, g1, w_in, b_in, qg, kg, seq, sb_width, tm=1024, tn=1536):
    t, d = x2.shape
    n = w_in.shape[1]
    nb = 3 * sb_width // tn
    nf = n // tn - nb
    assert nb * tn == 3 * sb_width and (nf + nb) * tn == n and tn % HEAD_DIM == 0
    heads = [kind for kind in "qkv" for _ in range(sb_width // HEAD_DIM)]
    per_tile = tn // HEAD_DIM
    kinds = tuple(tuple(heads[b * per_tile:(b + 1) * per_tile]) for b in range(nb))
    tiles_per_batch = seq // tm
    kern = functools.partial(_inproj_kernel, nf=nf, kinds=kinds)
    wcol = lambda j: (j + nb) % (nf + nb)
    return pl.pallas_call(
        kern,
        grid=(t // tm, nf + nb),
        in_specs=[pl.BlockSpec((tm, d), lambda i, j: (i, 0)),
                  pl.BlockSpec((None, N_MOD, d), lambda i, j: (i // tiles_per_batch, 0, 0)),
                  pl.BlockSpec((1, d), lambda i, j: (0, 0)),
                  pl.BlockSpec((d, tn), lambda i, j: (0, wcol(j))),
                  pl.BlockSpec((1, tn), lambda i, j: (0, wcol(j))),
                  pl.BlockSpec((1, HEAD_DIM), lambda i, j: (0, 0)),
                  pl.BlockSpec((1, HEAD_DIM), lambda i, j: (0, 0))],
        out_specs=[pl.BlockSpec((tm, tn), lambda i, j: (i, jnp.maximum(j - nf, 0))),
                   pl.BlockSpec((tm, tn), lambda i, j: (i, jnp.minimum(j, nf)))],
        out_shape=[jax.ShapeDtypeStruct((t, nb * tn), BF16),
                   jax.ShapeDtypeStruct((t, (nf + 1) * tn), F32)],
        scratch_shapes=[pltpu.VMEM((tm, d), BF16)],
        compiler_params=_params(("arbitrary", "arbitrary")),
        name="inproj",
    )(x2, mod, g1, w_in, b_in, qg, kg)


def _cumsum_matrix(width):
    j = jnp.arange(width)
    m = jnp.concatenate([(j[:, None] >= j[None, :]).astype(BF16), jnp.ones((width, HEAD_DIM), BF16)], axis=1)
    return jnp.concatenate([m, m], axis=0)


def _attn_kernel(q_ref, k_ref, v_ref, m_ref, *refs, n_cast, qb, hb, first_fast, scale):
    cast_in, o_ref, cast_out = refs[:n_cast], refs[n_cast], refs[n_cast + 1:2 * n_cast + 1]
    carry_scr, acc_scr, mx_ref = refs[2 * n_cast + 1:]
    for src, dst in zip(cast_in, cast_out):
        dst[...] = src[...].astype(dst.dtype)

    i = pl.program_id(2)
    blk = HEAD_DIM
    row = lax.broadcasted_iota(jnp.int32, (blk, blk), 0)
    col = lax.broadcasted_iota(jnp.int32, (blk, blk), 1)
    causal = col < row

    def tiles(jobs, nb, diag, first, mx_mode=None):
        sl = lambda n: slice(n * blk, (n + 1) * blk)
        zs = [lax.dot_general(q_ref[sl(s), sl(h)], k_ref[pl.ds(kstart, nb * blk), sl(h)],
                              (((1,), (1,)), ((), ())), preferred_element_type=F32) * scale
              for s, h, kstart in jobs]
        lhs = []
        for z in zs:
            lom = -(jnp.maximum(z, 0.0) + jnp.log(1.0 + jnp.exp(-jnp.abs(z))))
            row_lhs = []
            for c in range(nb):
                lom_c = jnp.where(causal, lom[:, sl(c)], 0.0) if (diag and c == nb - 1) else lom[:, sl(c)]
                hi = lom_c.astype(BF16)
                lo = (lom_c - hi.astype(F32)).astype(BF16)
                row_lhs.append(jnp.concatenate([hi, lo], axis=1))
            lhs.append(row_lhs)
        boths = [[_dot(a, m_ref[...]) for a in row_lhs] for row_lhs in lhs]
        befores, carries = [], []
        for (s, h, _), both in zip(jobs, boths):
            carry = None if first else carry_scr[s, h]
            before = [None] * nb
            for c in reversed(range(nb)):
                before[c] = carry
                carry = both[c][:, blk:] if carry is None else carry + both[c][:, blk:]
            carry_scr[s, h] = carry
            befores.append(before)
            carries.append(carry)
        if mx_mode is not None:
            worst = jnp.max(functools.reduce(jnp.maximum, carries))
            mx_ref[0] = worst if mx_mode == "set" else jnp.maximum(mx_ref[0], worst)
        ws = []
        for z, both, before in zip(zs, boths, befores):
            w = [None] * nb
            for c in range(nb):
                logw = z[:, sl(c)] + both[c][:, :blk]
                if before[c] is not None:
                    logw = logw + before[c]
                e = jnp.exp(logw)
                w[c] = (jnp.where(causal, e, 0.0) if (diag and c == nb - 1) else e).astype(BF16)
            ws.append(w[0] if nb == 1 else jnp.concatenate(w, axis=1))
        for (s, h, kstart), w in zip(jobs, ws):
            pv = _dot(w, v_ref[pl.ds(kstart, nb * blk), sl(h)])
            acc = pv if first else acc_scr[s, h] + pv
            acc_scr[s, h] = acc
            o_ref[sl(s), sl(h)] = acc.astype(o_ref.dtype)

    g0 = i * qb
    group = min(qb, 4)
    fast = i >= first_fast

    def jobs_for(s, block):
        kstart = pl.multiple_of(block * blk, blk)
        return [(s, h, kstart) for h in range(hb)]

    @pl.when(fast)
    def _():
        for s0 in range(0, qb, group):
            tiles([job for s in range(s0, s0 + group) for job in jobs_for(s, g0 + s - 2)], 3, True, True,
                  "set" if s0 == 0 else "max")

    @pl.when(jnp.logical_not(fast))
    def _():
        for s0 in range(0, qb, group):
            tiles([job for s in range(s0, s0 + group) for job in jobs_for(s, g0 + s)], 1, True, True,
                  "set" if s0 == 0 else "max")

    jbase = g0 - jnp.where(fast, 3, 1)
    nmax = jbase + qb - 1

    def cond(st):
        n, mx = st
        return jnp.logical_and(n <= nmax, mx > -F32_EXP_UNDERFLOW)

    def body(st):
        n, _ = st
        for s in range(qb):
            j = jbase + s - n

            @pl.when(j >= 0)
            def _():
                tiles(jobs_for(s, j), 1, False, False)

        return n + 1, jnp.max(carry_scr[...])

    lax.while_loop(cond, body, (jnp.int32(0), mx_ref[0]))


def _attn(qkv, batch, seq, n_heads, to_bf16=(), qb=4, hb=4):
    t = qkv.shape[0]
    blk = HEAD_DIM
    tq = qb * blk
    nq = seq // tq
    ng = n_heads // hb
    n_steps = batch * ng * nq
    first_fast = -(-2 // qb)
    kern = functools.partial(_attn_kernel, n_cast=len(to_bf16), qb=qb, hb=hb, first_fast=first_fast,
                             scale=1.0 / math.sqrt(HEAD_DIM))
    once = pl.Buffered(1)
    m = _cumsum_matrix(blk)
    bf16_rows = 2 * SUBLANES
    for w in to_bf16:
        assert w.shape[0] % (n_steps * bf16_rows) == 0, "row slices must be whole bf16 tiles"
    slices = [pl.BlockSpec((w.shape[0] // n_steps, w.shape[1]), lambda b, g, i: ((b * ng + g) * nq + i, 0))
              for w in to_bf16]
    att, *copies = pl.pallas_call(
        kern,
        grid=(batch, ng, nq),
        in_specs=[pl.BlockSpec((tq, hb * blk), lambda b, g, i: (b * nq + i, g)),
                  pl.BlockSpec((seq, hb * blk), lambda b, g, i: (b, ng + g), pipeline_mode=once),
                  pl.BlockSpec((seq, hb * blk), lambda b, g, i: (b, 2 * ng + g), pipeline_mode=once),
                  pl.BlockSpec(m.shape, lambda b, g, i: (0, 0), pipeline_mode=once)] + slices,
        out_specs=[pl.BlockSpec((tq, hb * blk), lambda b, g, i: (b * nq + i, g))] + slices,
        out_shape=[jax.ShapeDtypeStruct((t, n_heads * blk), BF16)]
        + [jax.ShapeDtypeStruct(w.shape, BF16) for w in to_bf16],
        scratch_shapes=[pltpu.VMEM((qb, hb, blk, blk), F32), pltpu.VMEM((qb, hb, blk, blk), F32),
                        pltpu.SMEM((1,), F32)],
        compiler_params=_params(("arbitrary", "arbitrary", "arbitrary")),
        name="attn",
    )(qkv, qkv, qkv, m, *to_bf16)
    return att, copies


def _sigmoid(x):
    return 0.5 * jnp.tanh(0.5 * x) + 0.5


def _lru_phases(x_ref, gate_ref, cw_ref, cb_ref, wg_ref, bg_ref, lam_ref, out_ref, xs, us, g_s, a_s, b_s, h_st,
                *, tt, conv_w, blk, pitch, n_chunks, gate_rows):
    hist = SUBLANES
    chunk = tt // n_chunks
    step = lambda ref, t: ref.at[pl.ds(t, SUBLANES, stride=pitch), :]
    slab = lambda n, off=0: slice(n * pitch + off, n * pitch + off + tt)
    st = {}

    def conv(c):
        if c == 0:
            for n in range(SUBLANES):
                xs[slab(n, hist), :] = x_ref[:, n * blk:(n + 1) * blk]
            st["taps"] = [cw_ref[k] for k in range(conv_w)]
            st["win"] = [step(xs, hist - (conv_w - 1) + k)[...] for k in range(conv_w - 1)]
        win = st["win"]
        for t in range(c * chunk, (c + 1) * chunk):
            win = win + [step(xs, hist + t)[...]]
            u = cb_ref[...]
            for k in range(conv_w):
                u = u + st["taps"][k] * win[k]
            step(us, t)[...] = u
            win = win[1:]
        st["win"] = win
        if c == n_chunks - 1:
            for n in range(SUBLANES):
                xs[n * pitch:n * pitch + hist, :] = xs[n * pitch + tt:n * pitch + tt + hist, :]

    def gate_dot(n):
        g_s[n] = _dot(us[slab(n), :].astype(BF16), wg_ref[n]) + bg_ref[n]

    def gate(p):
        n, r = divmod(p, tt // gate_rows)
        rows = slice(n * pitch + r * gate_rows, n * pitch + (r + 1) * gate_rows)
        un = us[rows, :]
        t = jnp.tanh(g_s[n, r * gate_rows:(r + 1) * gate_rows, :])
        c = (0.5 * LRU_C) * jax.nn.log_sigmoid(lam_ref[:, n * blk:(n + 1) * blk])
        log_a = c * t[:, :blk] + c
        a = jnp.exp(log_a)
        a_s[rows, :] = a
        y = -jnp.tanh(log_a) * (a * a + 1.0)
        mult = jnp.where(y > 0.0, y * lax.rsqrt(y), 0.0)
        hun = 0.5 * un
        b_s[rows, :] = mult * (hun * t[:, blk:] + hun)

    def scan(c):
        h = h_st[...] if c == 0 else st["h"]
        for t in range(c * chunk, (c + 1) * chunk):
            h = step(a_s, t)[...] * h + step(b_s, t)[...]
            step(b_s, t)[...] = h
        st["h"] = h
        if c == n_chunks - 1:
            h_st[...] = h

    def out(n):
        cols = slice(n * blk, (n + 1) * blk)
        x = gate_ref[:, cols]
        hx = 0.5 * x
        gelu = hx * jnp.tanh(x * (GELU_K + (GELU_K * 0.044715) * (x * x))) + hx
        out_ref[:, cols] = (b_s[slab(n), :] * gelu).astype(out_ref.dtype)

    bind = lambda f, count: [functools.partial(f, c) for c in range(count)]
    return (bind(conv, n_chunks), bind(gate_dot, SUBLANES), bind(gate, SUBLANES * (tt // gate_rows)),
            bind(scan, n_chunks), bind(out, SUBLANES))


def _mix_kernel(lx_ref, lg_ref, cw_ref, cb_ref, wg_ref, bg_ref, lam_ref,
                att_ref, ga_ref, gb_ref, x_ref, mod_ref, wa_ref, wb_ref, wo_ref, g2_ref,
                x1_ref, h2_ref, xs, us, g_s, a_s, b_s, h_st, yb_next, yb_cur, merged_scr,
                *, n_tiles, tiles_per_batch, lru, cw_a, cw_o):
    i = pl.program_id(0)
    d = x_ref.shape[1]

    @pl.when(i == 0)
    def _():
        yb_next[...] = jnp.zeros_like(yb_next)

    @pl.when(jnp.minimum(i, n_tiles - 1) % tiles_per_batch == 0)
    def _():
        for n in range(SUBLANES):
            xs[n * lru["pitch"]:n * lru["pitch"] + SUBLANES, :] = jnp.zeros((SUBLANES, xs.shape[1]), F32)
        h_st[...] = jnp.zeros_like(h_st)

    yb_cur[...] = yb_next[...]
    conv, gate_dot, gate, scan, out = _lru_phases(lx_ref, lg_ref, cw_ref, cb_ref, wg_ref, bg_ref, lam_ref, yb_next,
                                                  xs, us, g_s, a_s, b_s, h_st, **lru)

    def branches(c):
        cs = slice(c * cw_a, (c + 1) * cw_a)
        ya = _dot(att_ref[...], wa_ref[:, cs])
        yb = _dot(yb_cur[...], wb_ref[:, cs])
        merged_scr[:, cs] = (_sigmoid(ga_ref[:, cs]) * ya + _sigmoid(gb_ref[:, cs]) * yb).astype(BF16)

    sq = []

    def outproj(c):
        cs = slice(c * cw_o, (c + 1) * cw_o)
        x1 = x_ref[:, cs] + mod_ref[2:3, cs] * _dot(merged_scr[...], wo_ref[:, cs])
        x1_ref[:, cs] = x1
        sq.append(jnp.sum(x1 * x1, axis=-1, keepdims=True))

    def norm2():
        gain = g2_ref[...] * (1.0 + mod_ref[4:5, :])
        y = x1_ref[...] * lax.rsqrt(sum(sq) * (1.0 / d) + NORM_EPS)
        h2_ref[...] = (y * gain + mod_ref[3:4, :]).astype(BF16)

    first = [functools.partial(branches, c) for c in range(d // cw_a)]
    second = [functools.partial(outproj, c) for c in range(d // cw_o)]
    for f in conv + gate_dot + gate + scan + out + first + second:
        f()
    norm2()


def _mix(att, rest, x2, mod, conv_w, conv_b, wg, bg, lam, wa, wb, wo, g2, seq, tm=256):
    t, d = x2.shape
    k, w = conv_w.shape
    n_blocks, blk = wg.shape[0], wg.shape[1]
    assert n_blocks == SUBLANES and blk == HEAD_DIM, "one 128-lane channel slab per sublane"
    wa_k = wa.shape[0]
    n_tiles = t // tm
    tiles_per_batch = seq // tm
    pitch = tm + 2 * SUBLANES if (tm // SUBLANES) % 2 else tm + SUBLANES
    lru = dict(tt=tm, conv_w=k, blk=blk, pitch=pitch, n_chunks=1, gate_rows=tm)
    kern = functools.partial(_mix_kernel, n_tiles=n_tiles, tiles_per_batch=tiles_per_batch, lru=lru,
                             cw_a=d, cw_o=d)
    const = lambda shape: pl.BlockSpec(shape, lambda i: (0,) * len(shape), pipeline_mode=pl.Buffered(1))
    li = lambda i: jnp.minimum(i, n_tiles - 1)
    mi = lambda i: jnp.maximum(i - 1, 0)
    slab = lambda: pltpu.VMEM((SUBLANES * pitch, blk), F32)
    return pl.pallas_call(
        kern,
        grid=(n_tiles + 1,),
        in_specs=[pl.BlockSpec((tm, w), lambda i: (li(i), 0)),
                  pl.BlockSpec((tm, w), lambda i: (li(i), 1)),
                  const((k, SUBLANES, blk)), const((SUBLANES, blk)),
                  const((n_blocks, blk, 2 * blk)), const((n_blocks, 1, 2 * blk)), const((1, w)),
                  pl.BlockSpec((tm, wa_k), lambda i: (mi(i), 0)),
                  pl.BlockSpec((tm, d), lambda i: (mi(i), 1)),
                  pl.BlockSpec((tm, d), lambda i: (mi(i), 2)),
                  pl.BlockSpec((tm, d), lambda i: (mi(i), 0)),
                  pl.BlockSpec((None, N_MOD, d), lambda i: (mi(i) // tiles_per_batch, 0, 0)),
                  const((wa_k, d)), const((w, d)), const((d, d)), const((1, d))],
        out_specs=[pl.BlockSpec((tm, d), lambda i: (mi(i), 0)),
                   pl.BlockSpec((tm, d), lambda i: (mi(i), 0))],
        out_shape=[jax.ShapeDtypeStruct((t, d), F32), jax.ShapeDtypeStruct((t, d), BF16)],
        scratch_shapes=[slab(), slab(), pltpu.VMEM((n_blocks, tm, 2 * blk), F32), slab(), slab(),
                        pltpu.VMEM((SUBLANES, blk), F32),
                        pltpu.VMEM((tm, w), BF16), pltpu.VMEM((tm, w), BF16), pltpu.VMEM((tm, d), BF16)],
        compiler_params=_params(("arbitrary",)),
        name="mix",
    )(rest, rest, conv_w.reshape(k, SUBLANES, blk), conv_b.reshape(SUBLANES, blk), wg, bg, lam,
      att, rest, rest, x2, mod, wa, wb, wo, g2)


def _mlp_kernel(h2_ref, x1_ref, mod_ref, w1_ref, w2_ref, o_ref, acc_scr, *, sub):
    f = pl.program_id(1)

    @pl.when(f == 0)
    def _():
        acc_scr[...] = jnp.zeros_like(acc_scr)

    for c in range(w1_ref.shape[1] // sub):
        cs = slice(c * sub, (c + 1) * sub)
        a = jnp.maximum(_dot(h2_ref[...], w1_ref[:, cs]), 0.0)
        acc_scr[...] += _dot((a * a).astype(BF16), w2_ref[cs, :])

    @pl.when(f == pl.num_programs(1) - 1)
    def _():
        o_ref[...] = x1_ref[...] + mod_ref[5:6, :] * acc_scr[...]


def _mlp(h2, x1, mod, w1, w2, seq, tm=512, tf=2048, sub=1024):
    t, d = x1.shape
    dff = w1.shape[1]
    tiles_per_batch = seq // tm
    return pl.pallas_call(
        functools.partial(_mlp_kernel, sub=sub),
        grid=(t // tm, dff // tf),
        in_specs=[pl.BlockSpec((tm, d), lambda i, f: (i, 0)),
                  pl.BlockSpec((tm, d), lambda i, f: (i, 0)),
                  pl.BlockSpec((None, N_MOD, d), lambda i, f: (i // tiles_per_batch, 0, 0)),
                  pl.BlockSpec((d, tf), lambda i, f: (0, f)),
                  pl.BlockSpec((tf, d), lambda i, f: (f, 0))],
        out_specs=pl.BlockSpec((tm, d), lambda i, f: (i, 0)),
        out_shape=jax.ShapeDtypeStruct((t, d), F32),
        scratch_shapes=[pltpu.VMEM((tm, d), F32)],
        compiler_params=_params(("arbitrary", "arbitrary"), vmem=MLP_VMEM_LIMIT),
        name="mlp",
    )(h2, x1, mod, w1, w2)


def kernel(x, c, w_ada, b_ada, norm1_g, w_in, b_in, q_norm_g, k_norm_g, conv_w, conv_b, w_rg, b_rg, w_ig, b_ig,
           lru_lambda, w_branch_a, w_branch_b, w_out, norm2_g, w_mlp_in, w_mlp_out):
    batch, seq, d = x.shape
    depth = w_ada.shape[0]
    n_blocks, blk = w_rg.shape[1], w_rg.shape[2]
    n_heads = w_branch_a.shape[1] // HEAD_DIM
    row = lambda v: v.reshape(1, -1)

    x2 = x.reshape(batch * seq, d)
    c_pad = jnp.zeros((SUBLANES, d), F32).at[:batch].set(c)
    for l in range(depth):
        mod = _ada(c_pad, w_ada[l], row(b_ada[l]))[:batch].reshape(batch, N_MOD, d)
        qkv, rest = _inproj(x2, mod, row(norm1_g[l]), w_in[l].astype(BF16), row(b_in[l]),
                            row(q_norm_g[l]), row(k_norm_g[l]), seq, n_heads * HEAD_DIM)
        att, (wo, w1, w2) = _attn(qkv, batch, seq, n_heads, (w_out[l], w_mlp_in[l], w_mlp_out[l]))
        wg = (0.5 * jnp.concatenate([w_rg[l], w_ig[l]], axis=-1)).astype(BF16)
        bg = 0.5 * jnp.concatenate([b_rg[l].reshape(n_blocks, 1, blk), b_ig[l].reshape(n_blocks, 1, blk)], axis=-1)
        x1, h2 = _mix(att, rest, x2, mod, conv_w[l], row(conv_b[l]), wg, bg, row(lru_lambda[l]),
                      w_branch_a[l].astype(BF16), w_branch_b[l].astype(BF16), wo, row(norm2_g[l]), seq)
        x2 = _mlp(h2, x1, mod, w1, w2, seq)
    return x2.reshape(batch, seq, d)
```

```python
import functools
import math

import jax
import jax.numpy as jnp
from jax import lax
from jax.experimental import pallas as pl
from jax.experimental.pallas import tpu as pltpu

F32 = jnp.float32
BF16 = jnp.bfloat16

NORM_EPS = 1e-6
LRU_C = 8.0
GELU_K = math.sqrt(2.0 / math.pi)
HEAD_DIM = 128
N_MOD = 6
SUBLANES = 8
F32_EXP_UNDERFLOW = 104.0
VMEM_LIMIT = 56 * 1024 * 1024
MLP_VMEM_LIMIT = 62 * 1024 * 1024


def _params(sem, vmem=VMEM_LIMIT):
    return pltpu.CompilerParams(dimension_semantics=sem, vmem_limit_bytes=vmem)


def _dot(a, b):
    return jnp.dot(a, b, preferred_element_type=F32)


def _ada_kernel(c_ref, w_ref, b_ref, o_ref):
    ca = jax.nn.silu(c_ref[...]).astype(BF16)
    o_ref[...] = _dot(ca, w_ref[...].astype(BF16)) + b_ref[...]


def _ada(c_pad, w_ada, b_ada, tn=1024):
    m, d = c_pad.shape
    n = w_ada.shape[1]
    return pl.pallas_call(
        _ada_kernel,
        grid=(n // tn,),
        in_specs=[pl.BlockSpec((m, d), lambda j: (0, 0)),
                  pl.BlockSpec((d, tn), lambda j: (0, j)),
                  pl.BlockSpec((1, tn), lambda j: (0, j))],
        out_specs=pl.BlockSpec((m, tn), lambda j: (0, j)),
        out_shape=jax.ShapeDtypeStruct((m, n), F32),
        compiler_params=_params(("arbitrary",)),
        name="ada",
    )(c_pad, w_ada, b_ada)


def _inproj_kernel(x_ref, mod_ref, g1_ref, w_ref, b_ref, qg_ref, kg_ref, obf_ref, of_ref, h_scr, *, nf, kinds):
    j = pl.program_id(1)

    @pl.when(j == 0)
    def _():
        x = x_ref[...]
        ms = jnp.mean(x * x, axis=-1, keepdims=True)
        gain = g1_ref[...] * (1.0 + mod_ref[1:2, :])
        h_scr[...] = (x * lax.rsqrt(ms + NORM_EPS) * gain + mod_ref[0:1, :]).astype(BF16)

    of_ref[...] = _dot(h_scr[...], w_ref[...]) + b_ref[...]

    for b, tile_kinds in enumerate(kinds):
        @pl.when(j == nf + b)
        def _():
            for hh, kind in enumerate(tile_kinds):
                sl = slice(hh * HEAD_DIM, (hh + 1) * HEAD_DIM)
                a = of_ref[:, sl]
                if kind != "v":
                    ms = jnp.mean(a * a, axis=-1, keepdims=True)
                    a = a * lax.rsqrt(ms + NORM_EPS) * (qg_ref if kind == "q" else kg_ref)[...]
                obf_ref[:, sl] = a.astype(BF16)


def _inproj(x2, mod, g1, w_in, b_in, qg, kg, seq, sb_width, tm=1024, tn=1536):
    t, d = x2.shape
    n = w_in.shape[1]
    nb = 3 * sb_width // tn
    nf = n // tn - nb
    assert nb * tn == 3 * sb_width and (nf + nb) * tn == n and tn % HEAD_DIM == 0
    heads = [kind for kind in "qkv" for _ in range(sb_width // HEAD_DIM)]
    per_tile = tn // HEAD_DIM
    kinds = tuple(tuple(heads[b * per_tile:(b + 1) * per_tile]) for b in range(nb))
    tiles_per_batch = seq // tm
    kern = functools.partial(_inproj_kernel, nf=nf, kinds=kinds)
    wcol = lambda j: (j + nb) % (nf + nb)
    return pl.pallas_call(
        kern,
        grid=(t // tm, nf + nb),
        in_specs=[pl.BlockSpec((tm, d), lambda i, j: (i, 0)),
                  pl.BlockSpec((None, N_MOD, d), lambda i, j: (i // tiles_per_batch, 0, 0)),
                  pl.BlockSpec((1, d), lambda i, j: (0, 0)),
                  pl.BlockSpec((d, tn), lambda i, j: (0, wcol(j))),
                  pl.BlockSpec((1, tn), lambda i, j: (0, wcol(j))),
                  pl.BlockSpec((1, HEAD_DIM), lambda i, j: (0, 0)),
                  pl.BlockSpec((1, HEAD_DIM), lambda i, j: (0, 0))],
        out_specs=[pl.BlockSpec((tm, tn), lambda i, j: (i, jnp.maximum(j - nf, 0))),
                   pl.BlockSpec((tm, tn), lambda i, j: (i, jnp.minimum(j, nf)))],
        out_shape=[jax.ShapeDtypeStruct((t, nb * tn), BF16),
                   jax.ShapeDtypeStruct((t, (nf + 1) * tn), F32)],
        scratch_shapes=[pltpu.VMEM((tm, d), BF16)],
        compiler_params=_params(("arbitrary", "arbitrary")),
        name="inproj",
    )(x2, mod, g1, w_in, b_in, qg, kg)


def _cumsum_matrix(width):
    j = jnp.arange(width)
    m = jnp.concatenate([(j[:, None] >= j[None, :]).astype(BF16), jnp.ones((width, HEAD_DIM), BF16)], axis=1)
    return jnp.concatenate([m, m], axis=0)


def _attn_kernel(q_ref, k_ref, v_ref, m_ref, *refs, n_cast, qb, hb, first_fast):
    cast_in, o_ref, cast_out = refs[:n_cast], refs[n_cast], refs[n_cast + 1:2 * n_cast + 1]
    carry_scr, acc_scr, mx_ref = refs[2 * n_cast + 1:]
    for src, dst in zip(cast_in, cast_out):
        dst[...] = src[...].astype(dst.dtype)

    i = pl.program_id(2)
    blk = HEAD_DIM
    row = lax.broadcasted_iota(jnp.int32, (blk, blk), 0)
    col = lax.broadcasted_iota(jnp.int32, (blk, blk), 1)
    causal = col < row

    def tiles(jobs, nb, diag, first, mx_mode=None):
        sl = lambda n: slice(n * blk, (n + 1) * blk)
        zs = [lax.dot_general(q_ref[sl(s), sl(h)], k_ref[pl.ds(kstart, nb * blk), sl(h)],
                              (((1,), (1,)), ((), ())), preferred_element_type=F32)
              for s, h, kstart in jobs]
        lhs = []
        for z in zs:
            lom = -(jnp.maximum(z, 0.0) + jnp.log(1.0 + jnp.exp(-jnp.abs(z))))
            row_lhs = []
            for c in range(nb):
                lom_c = jnp.where(causal, lom[:, sl(c)], 0.0) if (diag and c == nb - 1) else lom[:, sl(c)]
                hi = lom_c.astype(BF16)
                lo = (lom_c - hi.astype(F32)).astype(BF16)
                row_lhs.append(jnp.concatenate([hi, lo], axis=1))
            lhs.append(row_lhs)
        boths = [[_dot(a, m_ref[...]) for a in row_lhs] for row_lhs in lhs]
        befores, carries = [], []
        for (s, h, _), both in zip(jobs, boths):
            carry = None if first else carry_scr[s, h]
            before = [None] * nb
            for c in reversed(range(nb)):
                before[c] = carry
                carry = both[c][:, blk:] if carry is None else carry + both[c][:, blk:]
            carry_scr[s, h] = carry
            befores.append(before)
            carries.append(carry)
        if mx_mode is not None:
            worst = jnp.max(functools.reduce(jnp.maximum, carries))
            mx_ref[0] = worst if mx_mode == "set" else jnp.maximum(mx_ref[0], worst)
        ws = []
        for z, both, before in zip(zs, boths, befores):
            w = [None] * nb
            for c in range(nb):
                logw = z[:, sl(c)] + both[c][:, :blk]
                if before[c] is not None:
                    logw = logw + before[c]
                e = jnp.exp(logw)
                w[c] = (jnp.where(causal, e, 0.0) if (diag and c == nb - 1) else e).astype(BF16)
            ws.append(w[0] if nb == 1 else jnp.concatenate(w, axis=1))
        for (s, h, kstart), w in zip(jobs, ws):
            pv = _dot(w, v_ref[pl.ds(kstart, nb * blk), sl(h)])
            acc = pv if first else acc_scr[s, h] + pv
            acc_scr[s, h] = acc
            o_ref[sl(s), sl(h)] = acc.astype(o_ref.dtype)

    g0 = i * qb
    group = min(qb, 4)
    fast = i >= first_fast

    def jobs_for(s, block):
        kstart = pl.multiple_of(block * blk, blk)
        return [(s, h, kstart) for h in range(hb)]

    @pl.when(fast)
    def _():
        for s0 in range(0, qb, group):
            tiles([job for s in range(s0, s0 + group) for job in jobs_for(s, g0 + s - 2)], 3, True, True,
                  "set" if s0 == 0 else "max")

    @pl.when(jnp.logical_not(fast))
    def _():
        for s0 in range(0, qb, group):
            tiles([job for s in range(s0, s0 + group) for job in jobs_for(s, g0 + s)], 1, True, True,
                  "set" if s0 == 0 else "max")

    jbase = g0 - jnp.where(fast, 3, 1)
    nmax = jbase + qb - 1

    def cond(st):
        n, mx = st
        return jnp.logical_and(n <= nmax, mx > -F32_EXP_UNDERFLOW)

    def body(st):
        n, _ = st
        for s in range(qb):
            j = jbase + s - n

            @pl.when(j >= 0)
            def _():
                tiles(jobs_for(s, j), 1, False, False)

        return n + 1, jnp.max(carry_scr[...])

    lax.while_loop(cond, body, (jnp.int32(0), mx_ref[0]))


def _attn(qkv, batch, seq, n_heads, to_bf16=(), qb=4, hb=4):
    t = qkv.shape[0]
    blk = HEAD_DIM
    tq = qb * blk
    nq = seq // tq
    ng = n_heads // hb
    n_steps = batch * ng * nq
    first_fast = -(-2 // qb)
    kern = functools.partial(_attn_kernel, n_cast=len(to_bf16), qb=qb, hb=hb, first_fast=first_fast)
    once = pl.Buffered(1)
    m = _cumsum_matrix(blk)
    bf16_rows = 2 * SUBLANES
    for w in to_bf16:
        assert w.shape[0] % (n_steps * bf16_rows) == 0, "row slices must be whole bf16 tiles"
    slices = [pl.BlockSpec((w.shape[0] // n_steps, w.shape[1]), lambda b, g, i: ((b * ng + g) * nq + i, 0))
              for w in to_bf16]
    att, *copies = pl.pallas_call(
        kern,
        grid=(batch, ng, nq),
        in_specs=[pl.BlockSpec((tq, hb * blk), lambda b, g, i: (b * nq + i, g)),
                  pl.BlockSpec((seq, hb * blk), lambda b, g, i: (b, ng + g), pipeline_mode=once),
                  pl.BlockSpec((seq, hb * blk), lambda b, g, i: (b, 2 * ng + g), pipeline_mode=once),
                  pl.BlockSpec(m.shape, lambda b, g, i: (0, 0), pipeline_mode=once)] + slices,
        out_specs=[pl.BlockSpec((tq, hb * blk), lambda b, g, i: (b * nq + i, g))] + slices,
        out_shape=[jax.ShapeDtypeStruct((t, n_heads * blk), BF16)]
        + [jax.ShapeDtypeStruct(w.shape, BF16) for w in to_bf16],
        scratch_shapes=[pltpu.VMEM((qb, hb, blk, blk), F32), pltpu.VMEM((qb, hb, blk, blk), F32),
                        pltpu.SMEM((1,), F32)],
        compiler_params=_params(("arbitrary", "arbitrary", "arbitrary")),
        name="attn",
    )(qkv, qkv, qkv, m, *to_bf16)
    return att, copies


def _sigmoid(x):
    return 0.5 * jnp.tanh(0.5 * x) + 0.5


def _lru_phases(x_ref, gate_ref, cw_ref, cb_ref, wg_ref, bg_ref, lam_ref, out_ref, xs, us, g_s, a_s, b_s, h_st,
                *, tt, conv_w, blk, pitch, n_chunks, gate_rows):
    hist = SUBLANES
    chunk = tt // n_chunks
    step = lambda ref, t: ref.at[pl.ds(t, SUBLANES, stride=pitch), :]
    slab = lambda n, off=0: slice(n * pitch + off, n * pitch + off + tt)
    st = {}

    def conv(c):
        if c == 0:
            for n in range(SUBLANES):
                xs[slab(n, hist), :] = x_ref[:, n * blk:(n + 1) * blk]
            st["taps"] = [cw_ref[k] for k in range(conv_w)]
            st["win"] = [step(xs, hist - (conv_w - 1) + k)[...] for k in range(conv_w - 1)]
        win = st["win"]
        for t in range(c * chunk, (c + 1) * chunk):
            win = win + [step(xs, hist + t)[...]]
            u = cb_ref[...]
            for k in range(conv_w):
                u = u + st["taps"][k] * win[k]
            step(us, t)[...] = u
            win = win[1:]
        st["win"] = win
        if c == n_chunks - 1:
            for n in range(SUBLANES):
                xs[n * pitch:n * pitch + hist, :] = xs[n * pitch + tt:n * pitch + tt + hist, :]

    def gate_dot(n):
        g_s[n] = _dot(us[slab(n), :].astype(BF16), wg_ref[n]) + bg_ref[n]

    def gate(p):
        n, r = divmod(p, tt // gate_rows)
        rows = slice(n * pitch + r * gate_rows, n * pitch + (r + 1) * gate_rows)
        un = us[rows, :]
        t = jnp.tanh(g_s[n, r * gate_rows:(r + 1) * gate_rows, :])
        c = (0.5 * LRU_C) * jax.nn.log_sigmoid(lam_ref[:, n * blk:(n + 1) * blk])
        log_a = c * t[:, :blk] + c
        a = jnp.exp(log_a)
        a_s[rows, :] = a
        y = -jnp.tanh(log_a) * (a * a + 1.0)
        mult = jnp.where(y > 0.0, y * lax.rsqrt(y), 0.0)
        hun = 0.5 * un
        b_s[rows, :] = mult * (hun * t[:, blk:] + hun)

    def scan(c):
        h = h_st[...] if c == 0 else st["h"]
        for t in range(c * chunk, (c + 1) * chunk):
            h = step(a_s, t)[...] * h + step(b_s, t)[...]
            step(b_s, t)[...] = h
        st["h"] = h
        if c == n_chunks - 1:
            h_st[...] = h

    def out(n):
        cols = slice(n * blk, (n + 1) * blk)
        x = gate_ref[:, cols]
        hx = 0.5 * x
        gelu = hx * jnp.tanh(x * (GELU_K + (GELU_K * 0.044715) * (x * x))) + hx
        out_ref[:, cols] = (b_s[slab(n), :] * gelu).astype(out_ref.dtype)

    bind = lambda f, count: [functools.partial(f, c) for c in range(count)]
    return (bind(conv, n_chunks), bind(gate_dot, SUBLANES), bind(gate, SUBLANES * (tt // gate_rows)),
            bind(scan, n_chunks), bind(out, SUBLANES))


def _mix_kernel(lx_ref, lg_ref, cw_ref, cb_ref, wg_ref, bg_ref, lam_ref,
                att_ref, ga_ref, gb_ref, x_ref, mod_ref, wa_ref, wb_ref, wo_ref, g2_ref,
                x1_ref, h2_ref, xs, us, g_s, a_s, b_s, h_st, yb_next, yb_cur, merged_scr,
                *, n_tiles, tiles_per_batch, lru, cw_a, cw_o):
    i = pl.program_id(0)
    d = x_ref.shape[1]

    @pl.when(i == 0)
    def _():
        yb_next[...] = jnp.zeros_like(yb_next)

    @pl.when(jnp.minimum(i, n_tiles - 1) % tiles_per_batch == 0)
    def _():
        for n in range(SUBLANES):
            xs[n * lru["pitch"]:n * lru["pitch"] + SUBLANES, :] = jnp.zeros((SUBLANES, xs.shape[1]), F32)
        h_st[...] = jnp.zeros_like(h_st)

    yb_cur[...] = yb_next[...]
    conv, gate_dot, gate, scan, out = _lru_phases(lx_ref, lg_ref, cw_ref, cb_ref, wg_ref, bg_ref, lam_ref, yb_next,
                                                  xs, us, g_s, a_s, b_s, h_st, **lru)

    def branches(c):
        cs = slice(c * cw_a, (c + 1) * cw_a)
        ya = _dot(att_ref[...], wa_ref[:, cs])
        yb = _dot(yb_cur[...], wb_ref[:, cs])
        merged_scr[:, cs] = (_sigmoid(ga_ref[:, cs]) * ya + _sigmoid(gb_ref[:, cs]) * yb).astype(BF16)

    sq = []

    def outproj(c):
        cs = slice(c * cw_o, (c + 1) * cw_o)
        x1 = x_ref[:, cs] + mod_ref[2:3, cs] * _dot(merged_scr[...], wo_ref[:, cs])
        x1_ref[:, cs] = x1
        sq.append(jnp.sum(x1 * x1, axis=-1, keepdims=True))

    def norm2():
        gain = g2_ref[...] * (1.0 + mod_ref[4:5, :])
        y = x1_ref[...] * lax.rsqrt(sum(sq) * (1.0 / d) + NORM_EPS)
        h2_ref[...] = (y * gain + mod_ref[3:4, :]).astype(BF16)

    first = [functools.partial(branches, c) for c in range(d // cw_a)]
    second = [functools.partial(outproj, c) for c in range(d // cw_o)]
    for f in conv + gate_dot + gate + scan + out + first + second:
        f()
    norm2()


def _mix(att, rest, x2, mod, conv_w, conv_b, wg, bg, lam, wa, wb, wo, g2, seq, tm=256):
    t, d = x2.shape
    k, w = conv_w.shape
    n_blocks, blk = wg.shape[0], wg.shape[1]
    assert n_blocks == SUBLANES and blk == HEAD_DIM, "one 128-lane channel slab per sublane"
    wa_k = wa.shape[0]
    n_tiles = t // tm
    tiles_per_batch = seq // tm
    pitch = tm + 2 * SUBLANES if (tm // SUBLANES) % 2 else tm + SUBLANES
    lru = dict(tt=tm, conv_w=k, blk=blk, pitch=pitch, n_chunks=1, gate_rows=tm)
    kern = functools.partial(_mix_kernel, n_tiles=n_tiles, tiles_per_batch=tiles_per_batch, lru=lru,
                             cw_a=d, cw_o=d)
    const = lambda shape: pl.BlockSpec(shape, lambda i: (0,) * len(shape), pipeline_mode=pl.Buffered(1))
    li = lambda i: jnp.minimum(i, n_tiles - 1)
    mi = lambda i: jnp.maximum(i - 1, 0)
    slab = lambda: pltpu.VMEM((SUBLANES * pitch, blk), F32)
    return pl.pallas_call(
        kern,
        grid=(n_tiles + 1,),
        in_specs=[pl.BlockSpec((tm, w), lambda i: (li(i), 0)),
                  pl.BlockSpec((tm, w), lambda i: (li(i), 1)),
                  const((k, SUBLANES, blk)), const((SUBLANES, blk)),
                  const((n_blocks, blk, 2 * blk)), const((n_blocks, 1, 2 * blk)), const((1, w)),
                  pl.BlockSpec((tm, wa_k), lambda i: (mi(i), 0)),
                  pl.BlockSpec((tm, d), lambda i: (mi(i), 1)),
                  pl.BlockSpec((tm, d), lambda i: (mi(i), 2)),
                  pl.BlockSpec((tm, d), lambda i: (mi(i), 0)),
                  pl.BlockSpec((None, N_MOD, d), lambda i: (mi(i) // tiles_per_batch, 0, 0)),
                  const((wa_k, d)), const((w, d)), const((d, d)), const((1, d))],
        out_specs=[pl.BlockSpec((tm, d), lambda i: (mi(i), 0)),
                   pl.BlockSpec((tm, d), lambda i: (mi(i), 0))],
        out_shape=[jax.ShapeDtypeStruct((t, d), F32), jax.ShapeDtypeStruct((t, d), BF16)],
        scratch_shapes=[slab(), slab(), pltpu.VMEM((n_blocks, tm, 2 * blk), F32), slab(), slab(),
                        pltpu.VMEM((SUBLANES, blk), F32),
                        pltpu.VMEM((tm, w), BF16), pltpu.VMEM((tm, w), BF16), pltpu.VMEM((tm, d), BF16)],
        compiler_params=_params(("arbitrary",)),
        name="mix",
    )(rest, rest, conv_w.reshape(k, SUBLANES, blk), conv_b.reshape(SUBLANES, blk), wg, bg, lam,
      att, rest, rest, x2, mod, wa, wb, wo, g2)


def _mlp_kernel(h2_ref, x1_ref, mod_ref, w1_ref, w2_ref, o_ref, acc_scr, *, sub):
    f = pl.program_id(1)

    @pl.when(f == 0)
    def _():
        acc_scr[...] = jnp.zeros_like(acc_scr)

    for c in range(w1_ref.shape[1] // sub):
        cs = slice(c * sub, (c + 1) * sub)
        a = jnp.maximum(_dot(h2_ref[...], w1_ref[:, cs]), 0.0)
        acc_scr[...] += _dot((a * a).astype(BF16), w2_ref[cs, :])

    @pl.when(f == pl.num_programs(1) - 1)
    def _():
        o_ref[...] = x1_ref[...] + mod_ref[5:6, :] * acc_scr[...]


def _mlp(h2, x1, mod, w1, w2, seq, tm=512, tf=2048, sub=1024):
    t, d = x1.shape
    dff = w1.shape[1]
    tiles_per_batch = seq // tm
    return pl.pallas_call(
        functools.partial(_mlp_kernel, sub=sub),
        grid=(t // tm, dff // tf),
        in_specs=[pl.BlockSpec((tm, d), lambda i, f: (i, 0)),
                  pl.BlockSpec((tm, d), lambda i, f: (i, 0)),
                  pl.BlockSpec((None, N_MOD, d), lambda i, f: (i // tiles_per_batch, 0, 0)),
                  pl.BlockSpec((d, tf), lambda i, f: (0, f)),
                  pl.BlockSpec((tf, d), lambda i, f: (f, 0))],
        out_specs=pl.BlockSpec((tm, d), lambda i, f: (i, 0)),
        out_shape=jax.ShapeDtypeStruct((t, d), F32),
        scratch_shapes=[pltpu.VMEM((tm, d), F32)],
        compiler_params=_params(("arbitrary", "arbitrary"), vmem=MLP_VMEM_LIMIT),
        name="mlp",
    )(h2, x1, mod, w1, w2)


def kernel(x, c, w_ada, b_ada, norm1_g, w_in, b_in, q_norm_g, k_norm_g, conv_w, conv_b, w_rg, b_rg, w_ig, b_ig,
           lru_lambda, w_branch_a, w_branch_b, w_out, norm2_g, w_mlp_in, w_mlp_out):
    batch, seq, d = x.shape
    depth = w_ada.shape[0]
    n_blocks, blk = w_rg.shape[1], w_rg.shape[2]
    n_heads = w_branch_a.shape[1] // HEAD_DIM
    row = lambda v: v.reshape(1, -1)

    x2 = x.reshape(batch * seq, d)
    c_pad = jnp.zeros((SUBLANES, d), F32).at[:batch].set(c)
    for l in range(depth):
        mod = _ada(c_pad, w_ada[l], row(b_ada[l]))[:batch].reshape(batch, N_MOD, d)
        qkv, rest = _inproj(x2, mod, row(norm1_g[l]), w_in[l].astype(BF16), row(b_in[l]),
                            row(q_norm_g[l]) * (1.0 / math.sqrt(HEAD_DIM)), row(k_norm_g[l]), seq,
                            n_heads * HEAD_DIM)
        att, (wo, w1, w2) = _attn(qkv, batch, seq, n_heads, (w_out[l], w_mlp_in[l], w_mlp_out[l]))
        wg = (0.5 * jnp.concatenate([w_rg[l], w_ig[l]], axis=-1)).astype(BF16)
        bg = 0.5 * jnp.concatenate([b_rg[l].reshape(n_blocks, 1, blk), b_ig[l].reshape(n_blocks, 1, blk)], axis=-1)
        x1, h2 = _mix(att, rest, x2, mod, conv_w[l], row(conv_b[l]), wg, bg, row(lru_lambda[l]),
                      w_branch_a[l].astype(BF16), w_branch_b[l].astype(BF16), wo, row(norm2_g[l]), seq)
        x2 = _mlp(h2, x1, mod, w1, w2, seq)
    return x2.reshape(batch, seq, d)
```

```python
import functools
import math

import jax
import jax.numpy as jnp
from jax import lax
from jax.experimental import pallas as pl
from jax.experimental.pallas import tpu as pltpu

F32 = jnp.float32
BF16 = jnp.bfloat16

NORM_EPS = 1e-6
LRU_C = 8.0
GELU_K = math.sqrt(2.0 / math.pi)
HEAD_DIM = 128
N_MOD = 6
SUBLANES = 8
F32_EXP_UNDERFLOW = 104.0
VMEM_LIMIT = 56 * 1024 * 1024
MLP_VMEM_LIMIT = 62 * 1024 * 1024


def _params(sem, vmem=VMEM_LIMIT):
    return pltpu.CompilerParams(dimension_semantics=sem, vmem_limit_bytes=vmem)


def _dot(a, b):
    return jnp.dot(a, b, preferred_element_type=F32)


def _ada_kernel(c_ref, w_ref, b_ref, o_ref):
    ca = jax.nn.silu(c_ref[...]).astype(BF16)
    o_ref[...] = _dot(ca, w_ref[...].astype(BF16)) + b_ref[...]


def _ada(c_pad, w_ada, b_ada, tn=1024):
    m, d = c_pad.shape
    n = w_ada.shape[1]
    return pl.pallas_call(
        _ada_kernel,
        grid=(n // tn,),
        in_specs=[pl.BlockSpec((m, d), lambda j: (0, 0)),
                  pl.BlockSpec((d, tn), lambda j: (0, j)),
                  pl.BlockSpec((1, tn), lambda j: (0, j))],
        out_specs=pl.BlockSpec((m, tn), lambda j: (0, j)),
        out_shape=jax.ShapeDtypeStruct((m, n), F32),
        compiler_params=_params(("arbitrary",)),
        name="ada",
    )(c_pad, w_ada, b_ada)


def _inproj_kernel(x_ref, mod_ref, g1_ref, w_ref, b_ref, qg_ref, kg_ref, obf_ref, of_ref, h_scr, *, nf, kinds):
    j = pl.program_id(1)

    @pl.when(j == 0)
    def _():
        x = x_ref[...]
        ms = jnp.mean(x * x, axis=-1, keepdims=True)
        gain = g1_ref[...] * (1.0 + mod_ref[1:2, :])
        h_scr[...] = (x * lax.rsqrt(ms + NORM_EPS) * gain + mod_ref[0:1, :]).astype(BF16)

    of_ref[...] = _dot(h_scr[...], w_ref[...]) + b_ref[...]

    for b, tile_kinds in enumerate(kinds):
        @pl.when(j == nf + b)
        def _():
            for hh, kind in enumerate(tile_kinds):
                sl = slice(hh * HEAD_DIM, (hh + 1) * HEAD_DIM)
                a = of_ref[:, sl]
                if kind != "v":
                    ms = jnp.mean(a * a, axis=-1, keepdims=True)
                    a = a * lax.rsqrt(ms + NORM_EPS) * (qg_ref if kind == "q" else kg_ref)[...]
                obf_ref[:, sl] = a.astype(BF16)


def _inproj(x2, mod, g1, w_in, b_in, qg, kg, seq, sb_width, tm=1024, tn=1536):
    t, d = x2.shape
    n = w_in.shape[1]
    nb = 3 * sb_width // tn
    nf = n // tn - nb
    assert nb * tn == 3 * sb_width and (nf + nb) * tn == n and tn % HEAD_DIM == 0
    heads = [kind for kind in "qkv" for _ in range(sb_width // HEAD_DIM)]
    per_tile = tn // HEAD_DIM
    kinds = tuple(tuple(heads[b * per_tile:(b + 1) * per_tile]) for b in range(nb))
    tiles_per_batch = seq // tm
    kern = functools.partial(_inproj_kernel, nf=nf, kinds=kinds)
    wcol = lambda j: (j + nb) % (nf + nb)
    return pl.pallas_call(
        kern,
        grid=(t // tm, nf + nb),
        in_specs=[pl.BlockSpec((tm, d), lambda i, j: (i, 0)),
                  pl.BlockSpec((None, N_MOD, d), lambda i, j: (i // tiles_per_batch, 0, 0)),
                  pl.BlockSpec((1, d), lambda i, j: (0, 0)),
                  pl.BlockSpec((d, tn), lambda i, j: (0, wcol(j))),
                  pl.BlockSpec((1, tn), lambda i, j: (0, wcol(j))),
                  pl.BlockSpec((1, HEAD_DIM), lambda i, j: (0, 0)),
                  pl.BlockSpec((1, HEAD_DIM), lambda i, j: (0, 0))],
        out_specs=[pl.BlockSpec((tm, tn), lambda i, j: (i, jnp.maximum(j - nf, 0))),
                   pl.BlockSpec((tm, tn), lambda i, j: (i, jnp.minimum(j, nf)))],
        out_shape=[jax.ShapeDtypeStruct((t, nb * tn), BF16),
                   jax.ShapeDtypeStruct((t, (nf + 1) * tn), F32)],
        scratch_shapes=[pltpu.VMEM((tm, d), BF16)],
        compiler_params=_params(("arbitrary", "arbitrary")),
        name="inproj",
    )(x2, mod, g1, w_in, b_in, qg, kg)


def _cumsum_matrix(width):
    j = jnp.arange(width)
    m = jnp.concatenate([(j[:, None] >= j[None, :]).astype(BF16), jnp.ones((width, HEAD_DIM), BF16)], axis=1)
    return jnp.concatenate([m, m], axis=0)


def _attn_kernel(q_ref, k_ref, v_ref, m_ref, *refs, n_cast, qb, hb, first_fast, scale):
    cast_in, o_ref, cast_out = refs[:n_cast], refs[n_cast], refs[n_cast + 1:2 * n_cast + 1]
    carry_scr, acc_scr, exit_ref = refs[2 * n_cast + 1:]
    for src, dst in zip(cast_in, cast_out):
        dst[...] = src[...].astype(dst.dtype)

    i = pl.program_id(2)
    blk = HEAD_DIM
    row = lax.broadcasted_iota(jnp.int32, (blk, blk), 0)
    col = lax.broadcasted_iota(jnp.int32, (blk, blk), 1)
    causal = col < row

    def tiles(jobs, nb, diag, first, exit_mode=None):
        sl = lambda n: slice(n * blk, (n + 1) * blk)
        zs = [lax.dot_general(q_ref[sl(s), sl(h)], k_ref[pl.ds(kstart, nb * blk), sl(h)],
                              (((1,), (1,)), ((), ())), preferred_element_type=F32) * scale
              for s, h, kstart in jobs]
        lhs = []
        for z in zs:
            sp = jnp.maximum(z, 0.0) + jnp.log(1.0 + jnp.exp(-jnp.abs(z)))
            row_lhs = []
            for c in range(nb):
                sp_c = jnp.where(causal, sp[:, sl(c)], 0.0) if (diag and c == nb - 1) else sp[:, sl(c)]
                hi = sp_c.astype(BF16)
                lo = (sp_c - hi.astype(F32)).astype(BF16)
                row_lhs.append(jnp.concatenate([hi, lo], axis=1))
            lhs.append(row_lhs)
        boths = [[_dot(a, m_ref[...]) for a in row_lhs] for row_lhs in lhs]
        befores, carries = [], []
        for (s, h, _), both in zip(jobs, boths):
            carry = None if first else carry_scr[s, h]
            before = [None] * nb
            for c in reversed(range(nb)):
                before[c] = carry
                carry = both[c][:, blk:] if carry is None else carry + both[c][:, blk:]
            carry_scr[s, h] = carry
            befores.append(before)
            carries.append(carry)
        if exit_mode is not None:
            least = jnp.min(functools.reduce(jnp.minimum, carries))
            exit_ref[0] = least if exit_mode == "set" else jnp.minimum(exit_ref[0], least)
        ws = []
        for z, both, before in zip(zs, boths, befores):
            w = [None] * nb
            for c in range(nb):
                logw = z[:, sl(c)] - both[c][:, :blk]
                if before[c] is not None:
                    logw = logw - before[c]
                e = jnp.exp(logw)
                w[c] = (jnp.where(causal, e, 0.0) if (diag and c == nb - 1) else e).astype(BF16)
            ws.append(w[0] if nb == 1 else jnp.concatenate(w, axis=1))
        for (s, h, kstart), w in zip(jobs, ws):
            pv = _dot(w, v_ref[pl.ds(kstart, nb * blk), sl(h)])
            acc = pv if first else acc_scr[s, h] + pv
            acc_scr[s, h] = acc
            o_ref[sl(s), sl(h)] = acc.astype(o_ref.dtype)

    g0 = i * qb
    group = min(qb, 4)
    fast = i >= first_fast

    def jobs_for(s, block):
        kstart = pl.multiple_of(block * blk, blk)
        return [(s, h, kstart) for h in range(hb)]

    @pl.when(fast)
    def _():
        for s0 in range(0, qb, group):
            tiles([job for s in range(s0, s0 + group) for job in jobs_for(s, g0 + s - 2)], 3, True, True,
                  "set" if s0 == 0 else "min")

    @pl.when(jnp.logical_not(fast))
    def _():
        for s0 in range(0, qb, group):
            tiles([job for s in range(s0, s0 + group) for job in jobs_for(s, g0 + s)], 1, True, True,
                  "set" if s0 == 0 else "min")

    jbase = g0 - jnp.where(fast, 3, 1)
    nmax = jbase + qb - 1

    def cond(st):
        n, least = st
        return jnp.logical_and(n <= nmax, least < F32_EXP_UNDERFLOW)

    def body(st):
        n, _ = st
        for s in range(qb):
            j = jbase + s - n

            @pl.when(j >= 0)
            def _():
                tiles(jobs_for(s, j), 1, False, False)

        return n + 1, jnp.min(carry_scr[...])

    lax.while_loop(cond, body, (jnp.int32(0), exit_ref[0]))


def _attn(qkv, batch, seq, n_heads, to_bf16=(), qb=4, hb=4):
    t = qkv.shape[0]
    blk = HEAD_DIM
    tq = qb * blk
    nq = seq // tq
    ng = n_heads // hb
    n_steps = batch * ng * nq
    first_fast = -(-2 // qb)
    kern = functools.partial(_attn_kernel, n_cast=len(to_bf16), qb=qb, hb=hb, first_fast=first_fast,
                             scale=1.0 / math.sqrt(HEAD_DIM))
    once = pl.Buffered(1)
    m = _cumsum_matrix(blk)
    bf16_rows = 2 * SUBLANES
    for w in to_bf16:
        assert w.shape[0] % (n_steps * bf16_rows) == 0, "row slices must be whole bf16 tiles"
    slices = [pl.BlockSpec((w.shape[0] // n_steps, w.shape[1]), lambda b, g, i: ((b * ng + g) * nq + i, 0))
              for w in to_bf16]
    att, *copies = pl.pallas_call(
        kern,
        grid=(batch, ng, nq),
        in_specs=[pl.BlockSpec((tq, hb * blk), lambda b, g, i: (b * nq + i, g)),
                  pl.BlockSpec((seq, hb * blk), lambda b, g, i: (b, ng + g), pipeline_mode=once),
                  pl.BlockSpec((seq, hb * blk), lambda b, g, i: (b, 2 * ng + g), pipeline_mode=once),
                  pl.BlockSpec(m.shape, lambda b, g, i: (0, 0), pipeline_mode=once)] + slices,
        out_specs=[pl.BlockSpec((tq, hb * blk), lambda b, g, i: (b * nq + i, g))] + slices,
        out_shape=[jax.ShapeDtypeStruct((t, n_heads * blk), BF16)]
        + [jax.ShapeDtypeStruct(w.shape, BF16) for w in to_bf16],
        scratch_shapes=[pltpu.VMEM((qb, hb, blk, blk), F32), pltpu.VMEM((qb, hb, blk, blk), F32),
                        pltpu.SMEM((1,), F32)],
        compiler_params=_params(("arbitrary", "arbitrary", "arbitrary")),
        name="attn",
    )(qkv, qkv, qkv, m, *to_bf16)
    return att, copies


def _sigmoid(x):
    return 0.5 * jnp.tanh(0.5 * x) + 0.5


def _lru_phases(x_ref, gate_ref, cw_ref, cb_ref, wg_ref, bg_ref, lam_ref, out_ref, xs, us, g_s, a_s, b_s, h_st,
                *, tt, conv_w, blk, pitch, n_chunks, gate_rows):
    hist = SUBLANES
    chunk = tt // n_chunks
    step = lambda ref, t: ref.at[pl.ds(t, SUBLANES, stride=pitch), :]
    slab = lambda n, off=0: slice(n * pitch + off, n * pitch + off + tt)
    st = {}

    def conv(c):
        if c == 0:
            for n in range(SUBLANES):
                xs[slab(n, hist), :] = x_ref[:, n * blk:(n + 1) * blk]
            st["taps"] = [cw_ref[k] for k in range(conv_w)]
            st["win"] = [step(xs, hist - (conv_w - 1) + k)[...] for k in range(conv_w - 1)]
        win = st["win"]
        for t in range(c * chunk, (c + 1) * chunk):
            win = win + [step(xs, hist + t)[...]]
            u = cb_ref[...]
            for k in range(conv_w):
                u = u + st["taps"][k] * win[k]
            step(us, t)[...] = u
            win = win[1:]
        st["win"] = win
        if c == n_chunks - 1:
            for n in range(SUBLANES):
                xs[n * pitch:n * pitch + hist, :] = xs[n * pitch + tt:n * pitch + tt + hist, :]

    def gate_dot(n):
        g_s[n] = _dot(us[slab(n), :].astype(BF16), wg_ref[n]) + bg_ref[n]

    def gate(p):
        n, r = divmod(p, tt // gate_rows)
        rows = slice(n * pitch + r * gate_rows, n * pitch + (r + 1) * gate_rows)
        un = us[rows, :]
        t = jnp.tanh(g_s[n, r * gate_rows:(r + 1) * gate_rows, :])
        c = (0.5 * LRU_C) * jax.nn.log_sigmoid(lam_ref[:, n * blk:(n + 1) * blk])
        log_a = c * t[:, :blk] + c
        a = jnp.exp(log_a)
        a_s[rows, :] = a
        y = -jnp.tanh(log_a) * (a * a + 1.0)
        mult = jnp.where(y > 0.0, y * lax.rsqrt(y), 0.0)
        hun = 0.5 * un
        b_s[rows, :] = mult * (hun * t[:, blk:] + hun)

    def scan(c):
        h = h_st[...] if c == 0 else st["h"]
        for t in range(c * chunk, (c + 1) * chunk):
            h = step(a_s, t)[...] * h + step(b_s, t)[...]
            step(b_s, t)[...] = h
        st["h"] = h
        if c == n_chunks - 1:
            h_st[...] = h

    def out(n):
        cols = slice(n * blk, (n + 1) * blk)
        x = gate_ref[:, cols]
        hx = 0.5 * x
        gelu = hx * jnp.tanh(x * (GELU_K + (GELU_K * 0.044715) * (x * x))) + hx
        out_ref[:, cols] = (b_s[slab(n), :] * gelu).astype(out_ref.dtype)

    bind = lambda f, count: [functools.partial(f, c) for c in range(count)]
    return (bind(conv, n_chunks), bind(gate_dot, SUBLANES), bind(gate, SUBLANES * (tt // gate_rows)),
            bind(scan, n_chunks), bind(out, SUBLANES))


def _mix_kernel(lx_ref, lg_ref, cw_ref, cb_ref, wg_ref, bg_ref, lam_ref,
                att_ref, ga_ref, gb_ref, x_ref, mod_ref, wa_ref, wb_ref, wo_ref, g2_ref,
                x1_ref, h2_ref, xs, us, g_s, a_s, b_s, h_st, yb_next, yb_cur, merged_scr,
                *, n_tiles, tiles_per_batch, lru, cw_a, cw_o):
    i = pl.program_id(0)
    d = x_ref.shape[1]

    @pl.when(i == 0)
    def _():
        yb_next[...] = jnp.zeros_like(yb_next)

    @pl.when(jnp.minimum(i, n_tiles - 1) % tiles_per_batch == 0)
    def _():
        for n in range(SUBLANES):
            xs[n * lru["pitch"]:n * lru["pitch"] + SUBLANES, :] = jnp.zeros((SUBLANES, xs.shape[1]), F32)
        h_st[...] = jnp.zeros_like(h_st)

    yb_cur[...] = yb_next[...]
    conv, gate_dot, gate, scan, out = _lru_phases(lx_ref, lg_ref, cw_ref, cb_ref, wg_ref, bg_ref, lam_ref, yb_next,
                                                  xs, us, g_s, a_s, b_s, h_st, **lru)

    def branches(c):
        cs = slice(c * cw_a, (c + 1) * cw_a)
        ya = _dot(att_ref[...], wa_ref[:, cs])
        yb = _dot(yb_cur[...], wb_ref[:, cs])
        merged_scr[:, cs] = (_sigmoid(ga_ref[:, cs]) * ya + _sigmoid(gb_ref[:, cs]) * yb).astype(BF16)

    sq = []

    def outproj(c):
        cs = slice(c * cw_o, (c + 1) * cw_o)
        x1 = x_ref[:, cs] + mod_ref[2:3, cs] * _dot(merged_scr[...], wo_ref[:, cs])
        x1_ref[:, cs] = x1
        sq.append(jnp.sum(x1 * x1, axis=-1, keepdims=True))

    def norm2():
        gain = g2_ref[...] * (1.0 + mod_ref[4:5, :])
        y = x1_ref[...] * lax.rsqrt(sum(sq) * (1.0 / d) + NORM_EPS)
        h2_ref[...] = (y * gain + mod_ref[3:4, :]).astype(BF16)

    first = [functools.partial(branches, c) for c in range(d // cw_a)]
    second = [functools.partial(outproj, c) for c in range(d // cw_o)]
    for f in conv + gate_dot + gate + scan + out + first + second:
        f()
    norm2()


def _mix(att, rest, x2, mod, conv_w, conv_b, wg, bg, lam, wa, wb, wo, g2, seq, tm=256):
    t, d = x2.shape
    k, w = conv_w.shape
    n_blocks, blk = wg.shape[0], wg.shape[1]
    assert n_blocks == SUBLANES and blk == HEAD_DIM, "one 128-lane channel slab per sublane"
    wa_k = wa.shape[0]
    n_tiles = t // tm
    tiles_per_batch = seq // tm
    pitch = tm + 2 * SUBLANES if (tm // SUBLANES) % 2 else tm + SUBLANES
    lru = dict(tt=tm, conv_w=k, blk=blk, pitch=pitch, n_chunks=1, gate_rows=tm)
    kern = functools.partial(_mix_kernel, n_tiles=n_tiles, tiles_per_batch=tiles_per_batch, lru=lru,
                             cw_a=d, cw_o=d)
    const = lambda shape: pl.BlockSpec(shape, lambda i: (0,) * len(shape), pipeline_mode=pl.Buffered(1))
    li = lambda i: jnp.minimum(i, n_tiles - 1)
    mi = lambda i: jnp.maximum(i - 1, 0)
    slab = lambda: pltpu.VMEM((SUBLANES * pitch, blk), F32)
    return pl.pallas_call(
        kern,
        grid=(n_tiles + 1,),
        in_specs=[pl.BlockSpec((tm, w), lambda i: (li(i), 0)),
                  pl.BlockSpec((tm, w), lambda i: (li(i), 1)),
                  const((k, SUBLANES, blk)), const((SUBLANES, blk)),
                  const((n_blocks, blk, 2 * blk)), const((n_blocks, 1, 2 * blk)), const((1, w)),
                  pl.BlockSpec((tm, wa_k), lambda i: (mi(i), 0)),
                  pl.BlockSpec((tm, d), lambda i: (mi(i), 1)),
                  pl.BlockSpec((tm, d), lambda i: (mi(i), 2)),
                  pl.BlockSpec((tm, d), lambda i: (mi(i), 0)),
                  pl.BlockSpec((None, N_MOD, d), lambda i: (mi(i) // tiles_per_batch, 0, 0)),
                  const((wa_k, d)), const((w, d)), const((d, d)), const((1, d))],
        out_specs=[pl.BlockSpec((tm, d), lambda i: (mi(i), 0)),
                   pl.BlockSpec((tm, d), lambda i: (mi(i), 0))],
        out_shape=[jax.ShapeDtypeStruct((t, d), F32), jax.ShapeDtypeStruct((t, d), BF16)],
        scratch_shapes=[slab(), slab(), pltpu.VMEM((n_blocks, tm, 2 * blk), F32), slab(), slab(),
                        pltpu.VMEM((SUBLANES, blk), F32),
                        pltpu.VMEM((tm, w), BF16), pltpu.VMEM((tm, w), BF16), pltpu.VMEM((tm, d), BF16)],
        compiler_params=_params(("arbitrary",)),
        name="mix",
    )(rest, rest, conv_w.reshape(k, SUBLANES, blk), conv_b.reshape(SUBLANES, blk), wg, bg, lam,
      att, rest, rest, x2, mod, wa, wb, wo, g2)


def _mlp_kernel(h2_ref, x1_ref, mod_ref, w1_ref, w2_ref, o_ref, acc_scr, *, sub):
    f = pl.program_id(1)

    @pl.when(f == 0)
    def _():
        acc_scr[...] = jnp.zeros_like(acc_scr)

    for c in range(w1_ref.shape[1] // sub):
        cs = slice(c * sub, (c + 1) * sub)
        a = jnp.maximum(_dot(h2_ref[...], w1_ref[:, cs]), 0.0)
        acc_scr[...] += _dot((a * a).astype(BF16), w2_ref[cs, :])

    @pl.when(f == pl.num_programs(1) - 1)
    def _():
        o_ref[...] = x1_ref[...] + mod_ref[5:6, :] * acc_scr[...]


def _mlp(h2, x1, mod, w1, w2, seq, tm=512, tf=2048, sub=1024):
    t, d = x1.shape
    dff = w1.shape[1]
    tiles_per_batch = seq // tm
    return pl.pallas_call(
        functools.partial(_mlp_kernel, sub=sub),
        grid=(t // tm, dff // tf),
        in_specs=[pl.BlockSpec((tm, d), lambda i, f: (i, 0)),
                  pl.BlockSpec((tm, d), lambda i, f: (i, 0)),
                  pl.BlockSpec((None, N_MOD, d), lambda i, f: (i // tiles_per_batch, 0, 0)),
                  pl.BlockSpec((d, tf), lambda i, f: (0, f)),
                  pl.BlockSpec((tf, d), lambda i, f: (f, 0))],
        out_specs=pl.BlockSpec((tm, d), lambda i, f: (i, 0)),
        out_shape=jax.ShapeDtypeStruct((t, d), F32),
        scratch_shapes=[pltpu.VMEM((tm, d), F32)],
        compiler_params=_params(("arbitrary", "arbitrary"), vmem=MLP_VMEM_LIMIT),
        name="mlp",
    )(h2, x1, mod, w1, w2)


def kernel(x, c, w_ada, b_ada, norm1_g, w_in, b_in, q_norm_g, k_norm_g, conv_w, conv_b, w_rg, b_rg, w_ig, b_ig,
           lru_lambda, w_branch_a, w_branch_b, w_out, norm2_g, w_mlp_in, w_mlp_out):
    batch, seq, d = x.shape
    depth = w_ada.shape[0]
    n_blocks, blk = w_rg.shape[1], w_rg.shape[2]
    n_heads = w_branch_a.shape[1] // HEAD_DIM
    row = lambda v: v.reshape(1, -1)

    x2 = x.reshape(batch * seq, d)
    c_pad = jnp.zeros((SUBLANES, d), F32).at[:batch].set(c)
    for l in range(depth):
        mod = _ada(c_pad, w_ada[l], row(b_ada[l]))[:batch].reshape(batch, N_MOD, d)
        qkv, rest = _inproj(x2, mod, row(norm1_g[l]), w_in[l].astype(BF16), row(b_in[l]),
                            row(q_norm_g[l]), row(k_norm_g[l]), seq, n_heads * HEAD_DIM)
        att, (wo, w1, w2) = _attn(qkv, batch, seq, n_heads, (w_out[l], w_mlp_in[l], w_mlp_out[l]))
        wg = (0.5 * jnp.concatenate([w_rg[l], w_ig[l]], axis=-1)).astype(BF16)
        bg = 0.5 * jnp.concatenate([b_rg[l].reshape(n_blocks, 1, blk), b_ig[l].reshape(n_blocks, 1, blk)], axis=-1)
        x1, h2 = _mix(att, rest, x2, mod, conv_w[l], row(conv_b[l]), wg, bg, row(lru_lambda[l]),
                      w_branch_a[l].astype(BF16), w_branch_b[l].astype(BF16), wo, row(norm2_g[l]), seq)
        x2 = _mlp(h2, x1, mod, w1, w2, seq)
    return x2.reshape(batch, seq, d)
```
